```python
import math
import jax, jax.numpy as jnp
from jax import lax
import numpy as np

D_MODEL = 1024
BATCH = 4
SEQ = 4096
DEPTH = 4

MEM_LEN = 256
EPS = 1e-6
BLOCK_Q = 128
CONV_W = 3
CONV_CH = D_MODEL // 2
DIFF_HEADS = 4
DIFF_HEAD_DIM = 64
DIFF_V_DIM = 2 * DIFF_HEAD_DIM
DIFF_QK = DIFF_HEADS * 2 * DIFF_HEAD_DIM
DIFF_WIDTH = DIFF_HEADS * DIFF_V_DIM
EVEN_IN = 3 * CONV_CH + 2 * DIFF_QK + DIFF_WIDTH
EVEN_OUT = CONV_CH + DIFF_WIDTH
SB_HEADS = 16
SB_HEAD_DIM = D_MODEL // SB_HEADS
X_HEADS = 4
X_HEAD_DIM = D_MODEL // X_HEADS
D_FF = 2816
N_EVEN = (DEPTH + 1) // 2
N_ODD = DEPTH // 2

kernel_name = "hybrid_conv_diffattn_stickbreak_trunk"


def rmsnorm(x, g):
    xf = x.astype(jnp.float32)
    y = xf * lax.rsqrt(jnp.mean(xf * xf, axis=-1, keepdims=True) + EPS)
    return (y * g.astype(jnp.float32)).astype(x.dtype)


def causal_dwconv(x, w):
    S = x.shape[1]
    xp = jnp.pad(x, ((0, 0), (CONV_W - 1, 0), (0, 0)))
    out = w[CONV_W - 1] * xp[:, CONV_W - 1:CONV_W - 1 + S]
    for k in range(CONV_W - 1):
        out = out + w[k] * xp[:, k:k + S]
    return out


def to_blocks(q):
    B, H, S, d = q.shape
    return q.reshape(B, H, S // BLOCK_Q, BLOCK_Q, d).transpose(2, 0, 1, 3, 4)


def from_blocks(o):
    NB, B, H, BQ, d = o.shape
    return o.transpose(1, 2, 0, 3, 4).reshape(B, H, NB * BQ, d)


def alibi_slopes(n_heads):
    return jnp.asarray(2.0 ** (-8.0 * np.arange(1, n_heads + 1) / n_heads), dtype=jnp.float32)


def diff_attention(q1, q2, k1, k2, v, lam):
    S = v.shape[2]
    nb = S // BLOCK_Q
    scale = DIFF_HEAD_DIM ** -0.5
    key_pos = jnp.arange(S)
    slopes = alibi_slopes(DIFF_HEADS)

    def block(args):
        i, qb1, qb2 = args
        q_pos = i * BLOCK_Q + jnp.arange(BLOCK_Q)
        dist = q_pos[:, None] - key_pos[None, :]
        causal = dist >= 0
        bias = -slopes[:, None, None] * jnp.abs(dist).astype(jnp.float32)

        def probs(qb, k):
            s = jnp.einsum('bhqd,bhkd->bhqk', qb, k).astype(jnp.float32) * scale + bias
            s = jnp.where(causal, s, -jnp.inf)
            return jax.nn.softmax(s, axis=-1)

        p = probs(qb1, k1) - lam * probs(qb2, k2)
        return jnp.einsum('bhqk,bhkd->bhqd', p.astype(v.dtype), v)

    o = lax.map(block, (jnp.arange(nb), to_blocks(q1), to_blocks(q2)))
    return from_blocks(o)


def stick_breaking_attention(q, k, v):
    S = v.shape[2]
    nb = S // BLOCK_Q
    scale = SB_HEAD_DIM ** -0.5
    key_pos = jnp.arange(S)

    def block(args):
        i, qb = args
        q_pos = i * BLOCK_Q + jnp.arange(BLOCK_Q)
        strict = key_pos[None, :] < q_pos[:, None]
        z = jnp.einsum('bhqd,bhkd->bhqk', qb, k).astype(jnp.float32) * scale
        log_beta = jax.nn.log_sigmoid(z)
        log_1m = jnp.where(strict, jax.nn.log_sigmoid(-z), 0.0)
        tail = lax.cumsum(log_1m, axis=3, reverse=True) - log_1m
        a = jnp.where(strict, jnp.exp(log_beta + tail), 0.0)
        return jnp.einsum('bhqk,bhkd->bhqd', a.astype(v.dtype), v)

    o = lax.map(block, (jnp.arange(nb), to_blocks(q)))
    return from_blocks(o)


def even_mixer(h, w_in, conv_a, q_gain, k_gain, lam_vecs, subln, w_out, lam_init):
    B, S, _ = h.shape
    proj = h @ w_in
    a_b, a_c, a_x, qd, kd, vd = jnp.split(
        proj, [CONV_CH, 2 * CONV_CH, 3 * CONV_CH, 3 * CONV_CH + DIFF_QK,
               3 * CONV_CH + 2 * DIFF_QK], axis=-1)
    y_a = a_b * causal_dwconv(a_c * a_x, conv_a)
    q = rmsnorm(qd.reshape(B, S, DIFF_HEADS, 2, DIFF_HEAD_DIM), q_gain)
    k = rmsnorm(kd.reshape(B, S, DIFF_HEADS, 2, DIFF_HEAD_DIM), k_gain)
    q = q.transpose(3, 0, 2, 1, 4)
    k = k.transpose(3, 0, 2, 1, 4)
    v = vd.reshape(B, S, DIFF_HEADS, DIFF_V_DIM).transpose(0, 2, 1, 3)
    lf = lam_vecs.astype(jnp.float32)
    lam = jnp.exp(jnp.sum(lf[0] * lf[1])) - jnp.exp(jnp.sum(lf[2] * lf[3])) + lam_init
    o = diff_attention(q[0], q[1], k[0], k[1], v, lam)
    o = rmsnorm(o, subln) * (1.0 - lam_init)
    y_b = o.transpose(0, 2, 1, 3).reshape(B, S, DIFF_WIDTH)
    return jnp.concatenate([y_a, y_b], axis=-1) @ w_out


def odd_mixer(h, w_qkv, w_out):
    B, S, _ = h.shape
    qkv = (h @ w_qkv).reshape(B, S, 3, SB_HEADS, SB_HEAD_DIM).transpose(2, 0, 3, 1, 4)
    o = stick_breaking_attention(qkv[0], qkv[1], qkv[2])
    return o.transpose(0, 2, 1, 3).reshape(B, S, D_MODEL) @ w_out


def mem_attention(h, mem_n, w_q, w_kv, q_gain, k_gain, w_o):
    B, S, _ = h.shape
    M = mem_n.shape[1]
    q = rmsnorm((h @ w_q).reshape(B, S, X_HEADS, X_HEAD_DIM), q_gain)
    kv = (mem_n @ w_kv).reshape(B, M, 2, X_HEADS, X_HEAD_DIM)
    k = rmsnorm(kv[:, :, 0], k_gain)
    v = kv[:, :, 1]
    s = jnp.einsum('bqhd,bkhd->bhqk', q, k).astype(jnp.float32) * (X_HEAD_DIM ** -0.5)
    p = jax.nn.softmax(s, axis=-1)
    o = jnp.einsum('bhqk,bkhd->bqhd', p.astype(v.dtype), v)
    return o.reshape(B, S, D_MODEL) @ w_o


def conv_ffn(h, w_up, w_conv, w_down):
    u = causal_dwconv(h @ w_up, w_conv)
    gate, val = jnp.split(u, 2, axis=-1)
    return (jax.nn.silu(gate) * val) @ w_down


def setup_inputs(seed: int = 0) -> dict:
    key = jax.random.key(seed)
    ks = jax.random.split(key, 24)
    f32 = jnp.float32

    def w(k, shape, fan_in):
        return jax.random.normal(k, shape, f32) * (fan_in ** -0.5)

    def gain(k, shape):
        return 1.0 + 0.02 * jax.random.normal(k, shape, f32)

    return {
        "x": jax.random.normal(ks[0], (BATCH, SEQ, D_MODEL), f32),
        "mem": jax.random.normal(ks[1], (BATCH, MEM_LEN, D_MODEL), f32),
        "norm_mix": gain(ks[2], (DEPTH, D_MODEL)),
        "norm_xattn": gain(ks[3], (DEPTH, D_MODEL)),
        "norm_mem": gain(ks[4], (DEPTH, D_MODEL)),
        "norm_ffn": gain(ks[5], (DEPTH, D_MODEL)),
        "even_w_in": w(ks[6], (N_EVEN, D_MODEL, EVEN_IN), D_MODEL),
        "even_conv": w(ks[7], (N_EVEN, CONV_W, CONV_CH), CONV_W),
        "even_q_gain": gain(ks[8], (N_EVEN, DIFF_HEAD_DIM)),
        "even_k_gain": gain(ks[9], (N_EVEN, DIFF_HEAD_DIM)),
        "even_lambda": 0.1 * jax.random.normal(ks[10], (N_EVEN, 4, DIFF_HEAD_DIM), f32),
        "even_subln": gain(ks[11], (N_EVEN, DIFF_V_DIM)),
        "even_w_out": w(ks[12], (N_EVEN, EVEN_OUT, D_MODEL), EVEN_OUT),
        "odd_w_qkv": w(ks[13], (N_ODD, D_MODEL, 3 * D_MODEL), D_MODEL),
        "odd_w_out": w(ks[14], (N_ODD, D_MODEL, D_MODEL), D_MODEL),
        "x_w_q": w(ks[15], (DEPTH, D_MODEL, D_MODEL), D_MODEL),
        "x_w_kv": w(ks[16], (DEPTH, D_MODEL, 2 * D_MODEL), D_MODEL),
        "x_q_gain": gain(ks[17], (DEPTH, X_HEAD_DIM)),
        "x_k_gain": gain(ks[18], (DEPTH, X_HEAD_DIM)),
        "x_w_out": w(ks[19], (DEPTH, D_MODEL, D_MODEL), D_MODEL),
        "ffn_w_up": w(ks[20], (DEPTH, D_MODEL, 2 * D_FF), D_MODEL),
        "ffn_conv": w(ks[21], (DEPTH, CONV_W, 2 * D_FF), CONV_W),
        "ffn_w_down": w(ks[22], (DEPTH, D_FF, D_MODEL), D_FF),
    }


def reference(x, mem, norm_mix, norm_xattn, norm_mem, norm_ffn,
              even_w_in, even_conv, even_q_gain, even_k_gain, even_lambda, even_subln, even_w_out,
              odd_w_qkv, odd_w_out,
              x_w_q, x_w_kv, x_q_gain, x_k_gain, x_w_out,
              ffn_w_up, ffn_conv, ffn_w_down):
    for l in range(DEPTH):
        h = rmsnorm(x, norm_mix[l])
        if l % 2 == 0:
            e = l // 2
            lam_init = 0.8 - 0.6 * math.exp(-0.3 * l)
            x = x + even_mixer(h, even_w_in[e], even_conv[e], even_q_gain[e], even_k_gain[e],
                               even_lambda[e], even_subln[e], even_w_out[e], lam_init)
        else:
            o = l // 2
            x = x + odd_mixer(h, odd_w_qkv[o], odd_w_out[o])
        x = x + mem_attention(rmsnorm(x, norm_xattn[l]), rmsnorm(mem, norm_mem[l]),
                              x_w_q[l], x_w_kv[l], x_q_gain[l], x_k_gain[l], x_w_out[l])
        x = x + conv_ffn(rmsnorm(x, norm_ffn[l]), ffn_w_up[l], ffn_conv[l], ffn_w_down[l])
    return x
```

```python
import functools
import math

import jax
import jax.numpy as jnp
from jax import lax
from jax.experimental import pallas as pl
from jax.experimental.pallas import tpu as pltpu

F32 = jnp.float32
BF16 = jnp.bfloat16

EPS = 1e-6
CONV_W = 3
LANES = 128
SUBLANES = 8
VMEM_LIMIT = 56 * 1024 * 1024

DIFF_HEADS = 4
DIFF_HEAD_DIM = 64
SB_HEADS = 16
SB_HEAD_DIM = 64
X_HEADS = 4


def _cparams(n_axes):
    return pltpu.CompilerParams(
        dimension_semantics=("arbitrary",) * n_axes,
        vmem_limit_bytes=VMEM_LIMIT,
    )


def _rms(x, g):
    ms = jnp.mean(x * x, axis=-1, keepdims=True)
    return x * lax.rsqrt(ms + EPS) * g


def _dot(a, b):
    return jnp.dot(a, b, preferred_element_type=F32)


def _dot_nt(a, b):
    return lax.dot_general(a, b, (((1,), (1,)), ((), ())), preferred_element_type=F32)


def _causal_conv3(y, prev, w):
    ye = jnp.concatenate([prev, y], axis=0)
    y1 = pltpu.roll(ye, 1, 0)[SUBLANES:]
    y2 = pltpu.roll(ye, 2, 0)[SUBLANES:]
    return w[2:3] * y + w[1:2] * y1 + w[0:1] * y2


def _norm_proj_kernel(x_ref, g_ref, w_ref, o_ref, *, cw):
    h = _rms(x_ref[...], g_ref[...]).astype(BF16)
    for c in range(0, w_ref.shape[1], cw):
        o_ref[:, c:c + cw] = _dot(h, w_ref[:, c:c + cw]).astype(o_ref.dtype)


def norm_proj(x, g, w, *, bm, cw=512):
    m, d = x.shape
    n = w.shape[1]
    return pl.pallas_call(
        functools.partial(_norm_proj_kernel, cw=cw),
        grid=(m // bm,),
        in_specs=[
            pl.BlockSpec((bm, d), lambda i: (i, 0)),
            pl.BlockSpec((1, d), lambda i: (0, 0)),
            pl.BlockSpec((d, n), lambda i: (0, 0)),
        ],
        out_specs=pl.BlockSpec((bm, n), lambda i: (i, 0)),
        out_shape=jax.ShapeDtypeStruct((m, n), BF16),
        compiler_params=_cparams(1),
        name="norm_proj",
    )(x, g.reshape(1, d), w)


def _even_proj_kernel(x_ref, g_ref, w_ref, cw_ref, o_ref, carry_ref, *, blocks_per_seq, cc):
    i = pl.program_id(0)
    h = _rms(x_ref[...], g_ref[...]).astype(BF16)
    a_b = _dot(h, w_ref[:, 0:cc])
    u = _dot(h, w_ref[:, cc:2 * cc]) * _dot(h, w_ref[:, 2 * cc:3 * cc])
    first = (i % blocks_per_seq) == 0
    prev = jnp.where(first, 0.0, carry_ref[...])
    o_ref[:, 0:cc] = (a_b * _causal_conv3(u, prev, cw_ref[...])).astype(o_ref.dtype)
    carry_ref[...] = u[u.shape[0] - SUBLANES:]
    for c in range(3 * cc, w_ref.shape[1], cc):
        o_ref[:, c - 2 * cc:c - cc] = _dot(h, w_ref[:, c:c + cc]).astype(o_ref.dtype)


def even_proj(x, g, w, conv_w, *, bm, seq):
    m, d = x.shape
    n = w.shape[1]
    cc = conv_w.shape[1]
    n_out = n - 2 * cc
    return pl.pallas_call(
        functools.partial(_even_proj_kernel, blocks_per_seq=seq // bm, cc=cc),
        grid=(m // bm,),
        in_specs=[
            pl.BlockSpec((bm, d), lambda i: (i, 0)),
            pl.BlockSpec((1, d), lambda i: (0, 0)),
            pl.BlockSpec((d, n), lambda i: (0, 0)),
            pl.BlockSpec((CONV_W, cc), lambda i: (0, 0)),
        ],
        out_specs=pl.BlockSpec((bm, n_out), lambda i: (i, 0)),
        out_shape=jax.ShapeDtypeStruct((m, n_out), BF16),
        scratch_shapes=[pltpu.VMEM((SUBLANES, cc), F32)],
        compiler_params=_cparams(1),
        name="even_proj",
    )(x, g.reshape(1, d), w, conv_w)


def _resid_matmul_kernel(*refs, n_lhs):
    x_ref = refs[0]
    lhs = refs[1:1 + n_lhs]
    w_ref = refs[1 + n_lhs]
    o_ref = refs[2 + n_lhs]
    acc = x_ref[...]
    k0 = 0
    for a_ref in lhs:
        k = a_ref.shape[1]
        acc = acc + _dot(a_ref[...], w_ref[k0:k0 + k, :])
        k0 += k
    o_ref[...] = acc


def resid_matmul(x, lhs_list, w, *, bm):
    m, d = x.shape
    in_specs = [pl.BlockSpec((bm, d), lambda i: (i, 0))]
    args = [x]
    for arr, cb, width in lhs_list:
        in_specs.append(pl.BlockSpec((bm, width), lambda i, cb=cb: (i, cb)))
        args.append(arr)
    in_specs.append(pl.BlockSpec(w.shape, lambda i: (0, 0)))
    args.append(w)
    return pl.pallas_call(
        functools.partial(_resid_matmul_kernel, n_lhs=len(lhs_list)),
        grid=(m // bm,),
        in_specs=in_specs,
        out_specs=pl.BlockSpec((bm, d), lambda i: (i, 0)),
        out_shape=jax.ShapeDtypeStruct((m, d), F32),
        compiler_params=_cparams(1),
        name="resid_matmul",
    )(*args)


def _half_rms(y, gain):
    lane = lax.broadcasted_iota(jnp.int32, (1, LANES), 1)
    lo = lane < DIFF_HEAD_DIM
    sq = y * y
    s_lo = jnp.sum(jnp.where(lo, sq, 0.0), axis=-1, keepdims=True)
    s_hi = jnp.sum(jnp.where(lo, 0.0, sq), axis=-1, keepdims=True)
    ms = jnp.where(lo, s_lo, s_hi) * (1.0 / DIFF_HEAD_DIM)
    return y * lax.rsqrt(ms + EPS) * gain


def _diff_attn_kernel(slopes_ref, lam_ref, qg_ref, kg_ref, sub_ref, q_ref, k_ref, v_ref,
                      o_ref, kn_ref, *, bq, lam_init, norm_rows):
    h = pl.program_id(1)
    i = pl.program_id(2)
    seq = k_ref.shape[0]

    @pl.when(i == 0)
    def _():
        def body(r, c):
            rows = pl.ds(pl.multiple_of(r * norm_rows, norm_rows), norm_rows)
            kn_ref[rows, :] = _half_rms(k_ref[rows, :].astype(F32), kg_ref[...]).astype(BF16)
            return c
        lax.fori_loop(0, seq // norm_rows, body, 0)

    slope = slopes_ref[h]
    lane = lax.broadcasted_iota(jnp.int32, (1, LANES), 1)
    lo = lane < DIFF_HEAD_DIM
    qn = _half_rms(q_ref[...].astype(F32), qg_ref[...])
    q1 = jnp.where(lo, qn, 0.0).astype(BF16)
    q2 = jnp.where(lo, 0.0, qn).astype(BF16)
    row = lax.broadcasted_iota(jnp.int32, (bq, bq), 0)
    col = lax.broadcasted_iota(jnp.int32, (bq, bq), 1)
    bias = (-slope) * (row - col).astype(F32)

    d0 = pl.multiple_of(i * bq, bq)
    kd = kn_ref[pl.ds(d0, bq), :]
    vd = v_ref[pl.ds(d0, bq), :]
    causal = row >= col

    def init(qh):
        s = jnp.where(causal, _dot_nt(qh, kd) + bias, -jnp.inf)
        m = jnp.max(s, axis=-1, keepdims=True)
        p = jnp.exp(s - m)
        return m, jnp.sum(p, axis=-1, keepdims=True), _dot(p.astype(BF16), vd)

    def step(qh, kj, vj, nb, m, l, acc):
        s = _dot_nt(qh, kj) + bias
        m_new = jnp.maximum(m, jnp.max(s, axis=-1, keepdims=True) + nb)
        alpha = jnp.exp(m - m_new)
        p = jnp.exp(s - (m_new - nb))
        l = alpha * l + jnp.sum(p, axis=-1, keepdims=True)
        acc = alpha * acc + _dot(p.astype(BF16), vj)
        return m_new, l, acc

    def body(j, carry):
        m1, l1, a1, m2, l2, a2 = carry
        r0 = pl.multiple_of(j * bq, bq)
        kj = kn_ref[pl.ds(r0, bq), :]
        vj = v_ref[pl.ds(r0, bq), :]
        nb = (-slope) * ((i - j) * bq).astype(F32)
        m1, l1, a1 = step(q1, kj, vj, nb, m1, l1, a1)
        m2, l2, a2 = step(q2, kj, vj, nb, m2, l2, a2)
        return m1, l1, a1, m2, l2, a2

    m1, l1, a1, m2, l2, a2 = lax.fori_loop(0, i, body, init(q1) + init(q2))

    lf = lam_ref[...]
    lam = (jnp.exp(jnp.sum(lf[0:1] * lf[1:2], axis=-1, keepdims=True))
           - jnp.exp(jnp.sum(lf[2:3] * lf[3:4], axis=-1, keepdims=True)) + lam_init)
    o = a1 / l1 - lam * (a2 / l2)
    o_ref[...] = (_rms(o, sub_ref[...]) * (1.0 - lam_init)).astype(o_ref.dtype)


def diff_attn(p, slopes, lam_vecs, q_gain2, k_gain2, subln, *, batch, seq, bq, lam_init,
              q_col, k_col, v_col):
    nq = seq // bq
    hd = 2 * DIFF_HEAD_DIM
    return pl.pallas_call(
        functools.partial(_diff_attn_kernel, bq=bq, lam_init=lam_init, norm_rows=min(seq, 512)),
        grid=(batch, DIFF_HEADS, nq),
        in_specs=[
            pl.BlockSpec(memory_space=pltpu.SMEM),
            pl.BlockSpec((4, DIFF_HEAD_DIM), lambda b, h, i: (0, 0)),
            pl.BlockSpec((1, hd), lambda b, h, i: (0, 0)),
            pl.BlockSpec((1, hd), lambda b, h, i: (0, 0)),
            pl.BlockSpec((1, hd), lambda b, h, i: (0, 0)),
            pl.BlockSpec((bq, hd), lambda b, h, i: (b * nq + i, q_col + h)),
            pl.BlockSpec((seq, hd), lambda b, h, i: (b, k_col + h)),
            pl.BlockSpec((seq, hd), lambda b, h, i: (b, v_col + h)),
        ],
        out_specs=pl.BlockSpec((bq, hd), lambda b, h, i: (b * nq + i, h)),
        out_shape=jax.ShapeDtypeStruct((batch * seq, DIFF_HEADS * hd), BF16),
        scratch_shapes=[pltpu.VMEM((seq, hd), BF16)],
        compiler_params=_cparams(3),
        name="diff_attn",
    )(slopes, lam_vecs, q_gain2, k_gain2, subln, p, p, p)


def _sb_attn_kernel(tri_ref, q_ref, k_ref, v_ref, o_ref, *, bq, scale):
    i = pl.program_id(2)
    lane = lax.broadcasted_iota(jnp.int32, (1, LANES), 1)
    lo = lane < SB_HEAD_DIM
    row = lax.broadcasted_iota(jnp.int32, (bq, bq), 0)
    col = lax.broadcasted_iota(jnp.int32, (bq, bq), 1)
    strict = col < row
    q = q_ref[...] * scale
    tri = tri_ref[...]

    def logs(z):
        lp = jnp.log(1.0 + jnp.exp(-jnp.abs(z)))
        lb = jnp.minimum(z, 0.0) - lp
        return lb, lb - z

    outs = []
    for head_lo in (True, False):
        qh = jnp.where(lo if head_lo else jnp.logical_not(lo), q, jnp.zeros_like(q))

        d0 = pl.multiple_of(i * bq, bq)
        lb, l1m = logs(_dot_nt(qh, k_ref[pl.ds(d0, bq), :]))
        l1m = jnp.where(strict, l1m, 0.0)
        tail = _dot(l1m.astype(BF16), tri)
        a = jnp.where(strict, jnp.exp(lb + tail), 0.0)
        acc0 = _dot(a.astype(BF16), v_ref[pl.ds(d0, bq), :])
        run0 = jnp.sum(l1m, axis=-1, keepdims=True)

        def body(t, carry, qh=qh):
            acc, run = carry
            r0 = pl.multiple_of((i - 1 - t) * bq, bq)
            lb, l1m = logs(_dot_nt(qh, k_ref[pl.ds(r0, bq), :]))
            tail = _dot(l1m.astype(BF16), tri) + run
            a = jnp.exp(lb + tail)
            acc = acc + _dot(a.astype(BF16), v_ref[pl.ds(r0, bq), :])
            return acc, run + jnp.sum(l1m, axis=-1, keepdims=True)

        acc, _ = lax.fori_loop(0, i, body, (acc0, run0))
        outs.append(acc)
    o_ref[...] = jnp.where(lo, outs[0], outs[1]).astype(o_ref.dtype)


def sb_attn(qkv, tri, *, batch, seq, bq):
    nq = seq // bq
    pairs = SB_HEADS // 2
    return pl.pallas_call(
        functools.partial(_sb_attn_kernel, bq=bq, scale=SB_HEAD_DIM ** -0.5),
        grid=(batch, pairs, nq),
        in_specs=[
            pl.BlockSpec((bq, bq), lambda b, h, i: (0, 0)),
            pl.BlockSpec((bq, LANES), lambda b, h, i: (b * nq + i, h)),
            pl.BlockSpec((seq, LANES), lambda b, h, i: (b, pairs + h)),
            pl.BlockSpec((seq, LANES), lambda b, h, i: (b, 2 * pairs + h)),
        ],
        out_specs=pl.BlockSpec((bq, LANES), lambda b, h, i: (b * nq + i, h)),
        out_shape=jax.ShapeDtypeStruct((batch * seq, pairs * LANES), BF16),
        compiler_params=_cparams(3),
        name="sb_attn",
    )(tri, qkv, qkv, qkv)


def _xattn_kernel(x_ref, q_ref, qg_ref, kg_ref, k_ref, v_ref, wo_ref, o_ref, kn_ref, oc_ref,
                  *, blocks_per_seq, hd):
    i = pl.program_id(0)

    @pl.when(i % blocks_per_seq == 0)
    def _():
        for h in range(X_HEADS):
            cols = slice(h * hd, (h + 1) * hd)
            kn_ref[:, cols] = _rms(k_ref[:, cols].astype(F32), kg_ref[...]).astype(BF16)

    for h in range(X_HEADS):
        cols = slice(h * hd, (h + 1) * hd)
        qh = _rms(q_ref[:, cols].astype(F32), qg_ref[...]).astype(BF16)
        s = _dot_nt(qh, kn_ref[:, cols])
        p = jnp.exp(s - jnp.max(s, axis=-1, keepdims=True))
        l = jnp.sum(p, axis=-1, keepdims=True)
        oc_ref[:, cols] = (_dot(p.astype(BF16), v_ref[:, cols]) / l).astype(BF16)
    o_ref[...] = x_ref[...] + _dot(oc_ref[...], wo_ref[...])


def xattn_out(x, q, q_gain, k_gain, kv, w_o, *, bm, seq, mem_len):
    m, d = x.shape
    hd = d // X_HEADS
    blocks_per_seq = seq // bm
    return pl.pallas_call(
        functools.partial(_xattn_kernel, blocks_per_seq=blocks_per_seq, hd=hd),
        grid=(m // bm,),
        in_specs=[
            pl.BlockSpec((bm, d), lambda i: (i, 0)),
            pl.BlockSpec((bm, d), lambda i: (i, 0)),
            pl.BlockSpec((1, hd), lambda i: (0, 0)),
            pl.BlockSpec((1, hd), lambda i: (0, 0)),
            pl.BlockSpec((mem_len, d), lambda i: (i // blocks_per_seq, 0)),
            pl.BlockSpec((mem_len, d), lambda i: (i // blocks_per_seq, 1)),
            pl.BlockSpec((d, d), lambda i: (0, 0)),
        ],
        out_specs=pl.BlockSpec((bm, d), lambda i: (i, 0)),
        out_shape=jax.ShapeDtypeStruct((m, d), F32),
        scratch_shapes=[pltpu.VMEM((mem_len, d), BF16), pltpu.VMEM((bm, d), BF16)],
        compiler_params=_cparams(1),
        name="xattn_out",
    )(x, q, q_gain, k_gain, kv, kv, w_o)


def _ffn_kernel(x_ref, g_ref, wg_ref, wv_ref, cg_ref, cv_ref, wd_ref, o_ref,
                h_ref, acc_ref, pg_ref, pv_ref, *, blocks_per_seq):
    i = pl.program_id(0)
    c = pl.program_id(1)
    bm = x_ref.shape[0]

    @pl.when(c == 0)
    def _():
        h_ref[...] = _rms(x_ref[...], g_ref[...]).astype(BF16)
        acc_ref[...] = jnp.zeros_like(acc_ref)

    first = (i % blocks_per_seq) == 0
    h = h_ref[...]

    def branch(w_ref, cw_ref, prev_ref):
        y = _dot(h, w_ref[...])
        prev = jnp.where(first, 0.0, prev_ref[c])
        prev_ref[c] = y[bm - SUBLANES:]
        return _causal_conv3(y, prev, cw_ref[...])

    gate = branch(wg_ref, cg_ref, pg_ref)
    val = branch(wv_ref, cv_ref, pv_ref)
    act = gate * (1.0 / (1.0 + jnp.exp(-gate))) * val
    acc_ref[...] += _dot(act.astype(BF16), wd_ref[...])

    @pl.when(c == pl.num_programs(1) - 1)
    def _():
        o_ref[...] = x_ref[...] + acc_ref[...]


def ffn(x, g, w_up, w_conv, w_down, *, bm, seq, cf):
    m, d = x.shape
    d_ff = w_down.shape[0]
    nc = d_ff // cf
    return pl.pallas_call(
        functools.partial(_ffn_kernel, blocks_per_seq=seq // bm),
        grid=(m // bm, nc),
        in_specs=[
            pl.BlockSpec((bm, d), lambda i, c: (i, 0)),
            pl.BlockSpec((1, d), lambda i, c: (0, 0)),
            pl.BlockSpec((d, cf), lambda i, c: (0, c)),
            pl.BlockSpec((d, cf), lambda i, c: (0, nc + c)),
            pl.BlockSpec((CONV_W, cf), lambda i, c: (0, c)),
            pl.BlockSpec((CONV_W, cf), lambda i, c: (0, nc + c)),
            pl.BlockSpec((cf, d), lambda i, c: (c, 0)),
        ],
        out_specs=pl.BlockSpec((bm, d), lambda i, c: (i, 0)),
        out_shape=jax.ShapeDtypeStruct((m, d), F32),
        scratch_shapes=[
            pltpu.VMEM((bm, d), BF16),
            pltpu.VMEM((bm, d), F32),
            pltpu.VMEM((nc, SUBLANES, cf), F32),
            pltpu.VMEM((nc, SUBLANES, cf), F32),
        ],
        compiler_params=_cparams(2),
        name="ffn",
    )(x, g.reshape(1, d), w_up, w_up, w_conv, w_conv, w_down)


def kernel(x, mem, norm_mix, norm_xattn, norm_mem, norm_ffn, even_w_in, even_conv, even_q_gain,
           even_k_gain, even_lambda, even_subln, even_w_out, odd_w_qkv, odd_w_out, x_w_q, x_w_kv,
           x_q_gain, x_k_gain, x_w_out, ffn_w_up, ffn_conv, ffn_w_down):
    batch, seq, d = x.shape
    mem_len = mem.shape[1]
    depth = norm_mix.shape[0]
    n = batch * seq
    bm = min(512, seq)
    bq = min(256, seq)

    xs = x.reshape(n, d)
    mems = mem.reshape(batch * mem_len, d)
    bf = lambda a: a.astype(BF16)

    slopes = jnp.asarray([2.0 ** (-8.0 * (h + 1) / DIFF_HEADS) for h in range(DIFF_HEADS)], F32)
    tri = (lax.broadcasted_iota(jnp.int32, (bq, bq), 0)
           > lax.broadcasted_iota(jnp.int32, (bq, bq), 1)).astype(BF16)
    conv_ch = even_conv.shape[2]
    hd_diff = 2 * DIFF_HEAD_DIM
    x_hd = d // X_HEADS

    for l in range(depth):
        if l % 2 == 0:
            e = l // 2
            lam_init = 0.8 - 0.6 * math.exp(-0.3 * l)
            p = even_proj(xs, norm_mix[l], bf(even_w_in[e]), even_conv[e], bm=bm, seq=seq)
            qcol = conv_ch // hd_diff
            o = diff_attn(
                p, slopes, even_lambda[e],
                jnp.tile(even_q_gain[e], 2).reshape(1, hd_diff) * (DIFF_HEAD_DIM ** -0.5),
                jnp.tile(even_k_gain[e], 2).reshape(1, hd_diff),
                even_subln[e].reshape(1, hd_diff),
                batch=batch, seq=seq, bq=bq, lam_init=lam_init,
                q_col=qcol, k_col=qcol + DIFF_HEADS, v_col=qcol + 2 * DIFF_HEADS)
            xs = resid_matmul(xs, [(p, 0, conv_ch), (o, 0, o.shape[1])], bf(even_w_out[e]), bm=bm)
        else:
            k = l // 2
            qkv = norm_proj(xs, norm_mix[l], bf(odd_w_qkv[k]), bm=bm)
            o = sb_attn(qkv, tri, batch=batch, seq=seq, bq=bq)
            xs = resid_matmul(xs, [(o, 0, d)], bf(odd_w_out[k]), bm=bm)

        q = norm_proj(xs, norm_xattn[l], bf(x_w_q[l]), bm=bm)
        kv = norm_proj(mems, norm_mem[l], bf(x_w_kv[l]), bm=min(512, batch * mem_len))
        xs = xattn_out(xs, q, x_q_gain[l].reshape(1, x_hd) * (x_hd ** -0.5),
                       x_k_gain[l].reshape(1, x_hd), kv, bf(x_w_out[l]),
                       bm=bm, seq=seq, mem_len=mem_len)
        xs = ffn(xs, norm_ffn[l], bf(ffn_w_up[l]), ffn_conv[l], bf(ffn_w_down[l]),
                 bm=bm, seq=seq, cf=256)
    return xs.reshape(batch, seq, d)
```

```python
import functools
import math

import jax
import jax.numpy as jnp
from jax import lax
from jax.experimental import pallas as pl
from jax.experimental.pallas import tpu as pltpu

F32 = jnp.float32
BF16 = jnp.bfloat16

EPS = 1e-6
CONV_W = 3
LANES = 128
SUBLANES = 8
VMEM_LIMIT = 56 * 1024 * 1024

DIFF_HEADS = 4
DIFF_HEAD_DIM = 64
SB_HEADS = 16
SB_HEAD_DIM = 64
X_HEADS = 4
LOG2E = 1.4426950408889634
MASKED = -1e30


def _cparams(n_axes, flags=None):
    return pltpu.CompilerParams(
        dimension_semantics=("arbitrary",) * n_axes,
        vmem_limit_bytes=VMEM_LIMIT,
        flags=flags,
    )


def _rms(x, g):
    ms = jnp.mean(x * x, axis=-1, keepdims=True)
    return x * lax.rsqrt(ms + EPS) * g


def _dot(a, b):
    return jnp.dot(a, b, preferred_element_type=F32)


def _dot_nt(a, b):
    return lax.dot_general(a, b, (((1,), (1,)), ((), ())), preferred_element_type=F32)


def _causal_conv3(y, prev, w):
    ye = jnp.concatenate([prev, y], axis=0)
    y1 = pltpu.roll(ye, 1, 0)[SUBLANES:]
    y2 = pltpu.roll(ye, 2, 0)[SUBLANES:]
    return w[2:3] * y + w[1:2] * y1 + w[0:1] * y2


def _norm_proj_kernel(x_ref, g_ref, w_ref, o_ref, *, cw):
    h = _rms(x_ref[...], g_ref[...]).astype(BF16)
    for c in range(0, w_ref.shape[1], cw):
        o_ref[:, c:c + cw] = _dot(h, w_ref[:, c:c + cw]).astype(o_ref.dtype)


def norm_proj(x, g, w, *, bm, cw=512):
    m, d = x.shape
    n = w.shape[1]
    return pl.pallas_call(
        functools.partial(_norm_proj_kernel, cw=cw),
        grid=(m // bm,),
        in_specs=[
            pl.BlockSpec((bm, d), lambda i: (i, 0)),
            pl.BlockSpec((1, d), lambda i: (0, 0)),
            pl.BlockSpec((d, n), lambda i: (0, 0)),
        ],
        out_specs=pl.BlockSpec((bm, n), lambda i: (i, 0)),
        out_shape=jax.ShapeDtypeStruct((m, n), BF16),
        compiler_params=_cparams(1),
        name="norm_proj",
    )(x, g.reshape(1, d), w)


def _even_proj_kernel(x_ref, g_ref, w_ref, cw_ref, o_ref, carry_ref, *, blocks_per_seq, cc):
    i = pl.program_id(0)
    h = _rms(x_ref[...], g_ref[...]).astype(BF16)
    a_b = _dot(h, w_ref[:, 0:cc])
    u = _dot(h, w_ref[:, cc:2 * cc]) * _dot(h, w_ref[:, 2 * cc:3 * cc])
    first = (i % blocks_per_seq) == 0
    prev = jnp.where(first, 0.0, carry_ref[...])
    o_ref[:, 0:cc] = (a_b * _causal_conv3(u, prev, cw_ref[...])).astype(o_ref.dtype)
    carry_ref[...] = u[u.shape[0] - SUBLANES:]
    for c in range(3 * cc, w_ref.shape[1], cc):
        o_ref[:, c - 2 * cc:c - cc] = _dot(h, w_ref[:, c:c + cc]).astype(o_ref.dtype)


def even_proj(x, g, w, conv_w, *, bm, seq):
    m, d = x.shape
    n = w.shape[1]
    cc = conv_w.shape[1]
    n_out = n - 2 * cc
    return pl.pallas_call(
        functools.partial(_even_proj_kernel, blocks_per_seq=seq // bm, cc=cc),
        grid=(m // bm,),
        in_specs=[
            pl.BlockSpec((bm, d), lambda i: (i, 0)),
            pl.BlockSpec((1, d), lambda i: (0, 0)),
            pl.BlockSpec((d, n), lambda i: (0, 0)),
            pl.BlockSpec((CONV_W, cc), lambda i: (0, 0)),
        ],
        out_specs=pl.BlockSpec((bm, n_out), lambda i: (i, 0)),
        out_shape=jax.ShapeDtypeStruct((m, n_out), BF16),
        scratch_shapes=[pltpu.VMEM((SUBLANES, cc), F32)],
        compiler_params=_cparams(1),
        name="even_proj",
    )(x, g.reshape(1, d), w, conv_w)


def _resid_matmul_kernel(*refs, n_lhs):
    x_ref = refs[0]
    lhs = refs[1:1 + n_lhs]
    w_ref = refs[1 + n_lhs]
    o_ref = refs[2 + n_lhs]
    acc = x_ref[...]
    k0 = 0
    for a_ref in lhs:
        k = a_ref.shape[1]
        acc = acc + _dot(a_ref[...], w_ref[k0:k0 + k, :])
        k0 += k
    o_ref[...] = acc


def resid_matmul(x, lhs_list, w, *, bm):
    m, d = x.shape
    in_specs = [pl.BlockSpec((bm, d), lambda i: (i, 0))]
    args = [x]
    for arr, cb, width in lhs_list:
        in_specs.append(pl.BlockSpec((bm, width), lambda i, cb=cb: (i, cb)))
        args.append(arr)
    in_specs.append(pl.BlockSpec(w.shape, lambda i: (0, 0)))
    args.append(w)
    return pl.pallas_call(
        functools.partial(_resid_matmul_kernel, n_lhs=len(lhs_list)),
        grid=(m // bm,),
        in_specs=in_specs,
        out_specs=pl.BlockSpec((bm, d), lambda i: (i, 0)),
        out_shape=jax.ShapeDtypeStruct((m, d), F32),
        compiler_params=_cparams(1),
        name="resid_matmul",
    )(*args)


def _half_rms(y, gain):
    lane = lax.broadcasted_iota(jnp.int32, (1, LANES), 1)
    lo = lane < DIFF_HEAD_DIM
    sq = y * y
    s_lo = jnp.sum(jnp.where(lo, sq, 0.0), axis=-1, keepdims=True)
    s_hi = jnp.sum(jnp.where(lo, 0.0, sq), axis=-1, keepdims=True)
    ms = jnp.where(lo, s_lo, s_hi) * (1.0 / DIFF_HEAD_DIM)
    return y * lax.rsqrt(ms + EPS) * gain


def _diff_attn_kernel(slopes_ref, lam_ref, qg_ref, kg_ref, sub_ref, q_ref, k_ref, v_ref,
                      o_ref, kn_ref, *, bq, lam_init, norm_rows):
    h = pl.program_id(1)
    i = pl.program_id(2)
    seq = k_ref.shape[0]

    @pl.when(i == 0)
    def _():
        def body(r, c):
            rows = pl.ds(pl.multiple_of(r * norm_rows, norm_rows), norm_rows)
            kn_ref[rows, :] = _half_rms(k_ref[rows, :].astype(F32), kg_ref[...]).astype(BF16)
            return c
        lax.fori_loop(0, seq // norm_rows, body, 0)

    slope = slopes_ref[h] * LOG2E
    lane = lax.broadcasted_iota(jnp.int32, (1, LANES), 1)
    lo = lane < DIFF_HEAD_DIM
    qn = _half_rms(q_ref[...].astype(F32), qg_ref[...])
    q1 = jnp.where(lo, qn, 0.0).astype(BF16)
    q2 = jnp.where(lo, 0.0, qn).astype(BF16)
    row = lax.broadcasted_iota(jnp.int32, (bq, bq), 0)
    col = lax.broadcasted_iota(jnp.int32, (bq, bq), 1)
    bias = (-slope) * (row - col).astype(F32)

    d0 = pl.multiple_of(i * bq, bq)
    kd = kn_ref[pl.ds(d0, bq), :]
    vd = v_ref[pl.ds(d0, bq), :]
    causal = row >= col

    def init(qh):
        s = jnp.where(causal, _dot_nt(qh, kd) + bias, -jnp.inf)
        m = jnp.max(s, axis=-1, keepdims=True)
        p = jnp.exp2(s - m)
        return m, jnp.sum(p, axis=-1, keepdims=True), _dot(p.astype(BF16), vd)

    def step(qh, kj, vj, nb, m, l, acc):
        s = _dot_nt(qh, kj) + bias
        m_new = jnp.maximum(m, jnp.max(s, axis=-1, keepdims=True) + nb)
        alpha = jnp.exp2(m - m_new)
        p = jnp.exp2(s - (m_new - nb))
        l = alpha * l + jnp.sum(p, axis=-1, keepdims=True)
        acc = alpha * acc + _dot(p.astype(BF16), vj)
        return m_new, l, acc

    def body(j, carry):
        m1, l1, a1, m2, l2, a2 = carry
        r0 = pl.multiple_of(j * bq, bq)
        kj = kn_ref[pl.ds(r0, bq), :]
        vj = v_ref[pl.ds(r0, bq), :]
        nb = (-slope) * ((i - j) * bq).astype(F32)
        m1, l1, a1 = step(q1, kj, vj, nb, m1, l1, a1)
        m2, l2, a2 = step(q2, kj, vj, nb, m2, l2, a2)
        return m1, l1, a1, m2, l2, a2

    m1, l1, a1, m2, l2, a2 = lax.fori_loop(0, i, body, init(q1) + init(q2))

    lf = lam_ref[...]
    lam = (jnp.exp(jnp.sum(lf[0:1] * lf[1:2], axis=-1, keepdims=True))
           - jnp.exp(jnp.sum(lf[2:3] * lf[3:4], axis=-1, keepdims=True)) + lam_init)
    o = a1 / l1 - lam * (a2 / l2)
    o_ref[...] = (_rms(o, sub_ref[...]) * (1.0 - lam_init)).astype(o_ref.dtype)


def diff_attn(p, slopes, lam_vecs, q_gain2, k_gain2, subln, *, batch, seq, bq, lam_init,
              q_col, k_col, v_col):
    nq = seq // bq
    hd = 2 * DIFF_HEAD_DIM
    return pl.pallas_call(
        functools.partial(_diff_attn_kernel, bq=bq, lam_init=lam_init, norm_rows=min(seq, 512)),
        grid=(batch, DIFF_HEADS, nq),
        in_specs=[
            pl.BlockSpec(memory_space=pltpu.SMEM),
            pl.BlockSpec((4, DIFF_HEAD_DIM), lambda b, h, i: (0, 0)),
            pl.BlockSpec((1, hd), lambda b, h, i: (0, 0)),
            pl.BlockSpec((1, hd), lambda b, h, i: (0, 0)),
            pl.BlockSpec((1, hd), lambda b, h, i: (0, 0)),
            pl.BlockSpec((bq, hd), lambda b, h, i: (b * nq + i, q_col + h)),
            pl.BlockSpec((seq, hd), lambda b, h, i: (b, k_col + h)),
            pl.BlockSpec((seq, hd), lambda b, h, i: (b, v_col + h)),
        ],
        out_specs=pl.BlockSpec((bq, hd), lambda b, h, i: (b * nq + i, h)),
        out_shape=jax.ShapeDtypeStruct((batch * seq, DIFF_HEADS * hd), BF16),
        scratch_shapes=[pltpu.VMEM((seq, hd), BF16)],
        compiler_params=_cparams(3),
        name="diff_attn",
    )(slopes, lam_vecs, q_gain2, k_gain2, subln, p, p, p)


def _sb_attn_kernel(tri_ref, q_ref, k_ref, v_ref, o_ref, *, bq, scale):
    bk = bq // 2
    i = pl.program_id(2)
    lane = lax.broadcasted_iota(jnp.int32, (1, LANES), 1)
    lo = lane < SB_HEAD_DIM
    row = lax.broadcasted_iota(jnp.int32, (bq, bq), 0)
    col = lax.broadcasted_iota(jnp.int32, (bq, bq), 1)
    mask_bias = jnp.where(col < row, 0.0, MASKED)
    q = q_ref[...] * scale
    qs = (jnp.where(lo, q, jnp.zeros_like(q)), jnp.where(lo, jnp.zeros_like(q), q))
    tri = tri_ref[...]

    def slab(r0, bias, qh, acc, run):
        rows = pl.ds(r0, bq)
        z = _dot_nt(qh, k_ref[rows, :])
        if bias is not None:
            z = z + bias
        lp = jnp.log(1.0 + jnp.exp2(jnp.abs(z) * (-LOG2E)))
        lb = jnp.minimum(z, 0.0) - lp
        l1m = lb - z
        s_lo = jnp.sum(l1m[:, :bk], axis=-1, keepdims=True)
        s_hi = jnp.sum(l1m[:, bk:], axis=-1, keepdims=True)
        l1m = l1m.astype(BF16)
        t_hi = _dot(l1m[:, bk:], tri) + run
        t_lo = _dot(l1m[:, :bk], tri) + (run + s_hi)
        a = jnp.exp(lb + jnp.concatenate([t_lo, t_hi], axis=1)).astype(BF16)
        return acc + _dot(a, v_ref[rows, :]), run + (s_hi + s_lo)

    d0 = pl.multiple_of(i * bq, bq)
    zero_acc = jnp.zeros((bq, LANES), F32)
    zero_run = jnp.zeros((bq, 1), F32)
    state = (slab(d0, mask_bias, qs[0], zero_acc, zero_run)
             + slab(d0, mask_bias, qs[1], zero_acc, zero_run))

    def body(t, carry):
        acc0, run0, acc1, run1 = carry
        r0 = pl.multiple_of((i - 1 - t) * bq, bq)
        return slab(r0, None, qs[0], acc0, run0) + slab(r0, None, qs[1], acc1, run1)

    acc0, _, acc1, _ = lax.fori_loop(0, i, body, state)
    o_ref[...] = jnp.where(lo, acc0, acc1).astype(o_ref.dtype)


def sb_attn(qkv, tri, *, batch, seq, bq):
    nq = seq // bq
    pairs = SB_HEADS // 2
    return pl.pallas_call(
        functools.partial(_sb_attn_kernel, bq=bq, scale=SB_HEAD_DIM ** -0.5),
        grid=(batch, pairs, nq),
        in_specs=[
            pl.BlockSpec((bq // 2, bq // 2), lambda b, h, i: (0, 0)),
            pl.BlockSpec((bq, LANES), lambda b, h, i: (b * nq + i, h)),
            pl.BlockSpec((seq, LANES), lambda b, h, i: (b, pairs + h)),
            pl.BlockSpec((seq, LANES), lambda b, h, i: (b, 2 * pairs + h)),
        ],
        out_specs=pl.BlockSpec((bq, LANES), lambda b, h, i: (b * nq + i, h)),
        out_shape=jax.ShapeDtypeStruct((batch * seq, pairs * LANES), BF16),
        compiler_params=_cparams(3),
        name="sb_attn",
    )(tri, qkv, qkv, qkv)


def _xattn_kernel(x_ref, q_ref, qg_ref, kg_ref, k_ref, v_ref, wo_ref, o_ref, kn_ref, oc_ref,
                  *, blocks_per_seq, hd):
    i = pl.program_id(0)

    @pl.when(i % blocks_per_seq == 0)
    def _():
        for h in range(X_HEADS):
            cols = slice(h * hd, (h + 1) * hd)
            kn_ref[:, cols] = _rms(k_ref[:, cols].astype(F32), kg_ref[...]).astype(BF16)

    for h in range(X_HEADS):
        cols = slice(h * hd, (h + 1) * hd)
        qh = _rms(q_ref[:, cols].astype(F32), qg_ref[...]).astype(BF16)
        s = _dot_nt(qh, kn_ref[:, cols])
        p = jnp.exp(s - jnp.max(s, axis=-1, keepdims=True))
        l = jnp.sum(p, axis=-1, keepdims=True)
        oc_ref[:, cols] = (_dot(p.astype(BF16), v_ref[:, cols]) / l).astype(BF16)
    o_ref[...] = x_ref[...] + _dot(oc_ref[...], wo_ref[...])


def xattn_out(x, q, q_gain, k_gain, kv, w_o, *, bm, seq, mem_len):
    m, d = x.shape
    hd = d // X_HEADS
    blocks_per_seq = seq // bm
    return pl.pallas_call(
        functools.partial(_xattn_kernel, blocks_per_seq=blocks_per_seq, hd=hd),
        grid=(m // bm,),
        in_specs=[
            pl.BlockSpec((bm, d), lambda i: (i, 0)),
            pl.BlockSpec((bm, d), lambda i: (i, 0)),
            pl.BlockSpec((1, hd), lambda i: (0, 0)),
            pl.BlockSpec((1, hd), lambda i: (0, 0)),
            pl.BlockSpec((mem_len, d), lambda i: (i // blocks_per_seq, 0)),
            pl.BlockSpec((mem_len, d), lambda i: (i // blocks_per_seq, 1)),
            pl.BlockSpec((d, d), lambda i: (0, 0)),
        ],
        out_specs=pl.BlockSpec((bm, d), lambda i: (i, 0)),
        out_shape=jax.ShapeDtypeStruct((m, d), F32),
        scratch_shapes=[pltpu.VMEM((mem_len, d), BF16), pltpu.VMEM((bm, d), BF16)],
        compiler_params=_cparams(1),
        name="xattn_out",
    )(x, q, q_gain, k_gain, kv, kv, w_o)


def _ffn_kernel(x_ref, g_ref, wu_ref, cw_ref, wd_ref, o_ref, pg_ref, pv_ref, *, blocks_per_seq, cf):
    i = pl.program_id(0)
    bm = x_ref.shape[0]
    d_ff = wd_ref.shape[0]
    first = (i % blocks_per_seq) == 0
    x = x_ref[...]
    h = _rms(x, g_ref[...]).astype(BF16)

    def branch(col, prev_ref, c):
        y = _dot(h, wu_ref[:, col:col + cf])
        prev = jnp.where(first, 0.0, prev_ref[c])
        prev_ref[c] = y[bm - SUBLANES:]
        return _causal_conv3(y, prev, cw_ref[:, col:col + cf])

    acc = x
    for c in range(d_ff // cf):
        gate = branch(c * cf, pg_ref, c)
        val = branch(d_ff + c * cf, pv_ref, c)
        act = gate * (1.0 / (1.0 + jnp.exp(-gate))) * val
        acc = acc + _dot(act.astype(BF16), wd_ref[c * cf:(c + 1) * cf, :])
    o_ref[...] = acc


def ffn(x, g, w_up, w_conv, w_down, *, bm, seq, cf):
    m, d = x.shape
    d_ff = w_down.shape[0]
    nc = d_ff // cf
    resident = dict(pipeline_mode=pl.Buffered(1))
    return pl.pallas_call(
        functools.partial(_ffn_kernel, blocks_per_seq=seq // bm, cf=cf),
        grid=(m // bm,),
        in_specs=[
            pl.BlockSpec((bm, d), lambda i: (i, 0)),
            pl.BlockSpec((1, d), lambda i: (0, 0)),
            pl.BlockSpec((d, 2 * d_ff), lambda i: (0, 0), **resident),
            pl.BlockSpec((CONV_W, 2 * d_ff), lambda i: (0, 0), **resident),
            pl.BlockSpec((d_ff, d), lambda i: (0, 0), **resident),
        ],
        out_specs=pl.BlockSpec((bm, d), lambda i: (i, 0)),
        out_shape=jax.ShapeDtypeStruct((m, d), F32),
        scratch_shapes=[
            pltpu.VMEM((nc, SUBLANES, cf), F32),
            pltpu.VMEM((nc, SUBLANES, cf), F32),
        ],
        compiler_params=_cparams(1),
        name="ffn",
    )(x, g.reshape(1, d), w_up, w_conv, w_down)


def kernel(x, mem, norm_mix, norm_xattn, norm_mem, norm_ffn, even_w_in, even_conv, even_q_gain,
           even_k_gain, even_lambda, even_subln, even_w_out, odd_w_qkv, odd_w_out, x_w_q, x_w_kv,
           x_q_gain, x_k_gain, x_w_out, ffn_w_up, ffn_conv, ffn_w_down):
    batch, seq, d = x.shape
    mem_len = mem.shape[1]
    depth = norm_mix.shape[0]
    n = batch * seq
    bm = min(512, seq)
    bq = min(512, seq)
    bk_sb = bq // 2

    xs = x.reshape(n, d)
    mems = mem.reshape(batch * mem_len, d)
    bf = lambda a: a.astype(BF16)

    slopes = jnp.asarray([2.0 ** (-8.0 * (h + 1) / DIFF_HEADS) for h in range(DIFF_HEADS)], F32)
    tri = (lax.broadcasted_iota(jnp.int32, (bk_sb, bk_sb), 0)
           > lax.broadcasted_iota(jnp.int32, (bk_sb, bk_sb), 1)).astype(BF16)
    conv_ch = even_conv.shape[2]
    hd_diff = 2 * DIFF_HEAD_DIM
    x_hd = d // X_HEADS

    for l in range(depth):
        if l % 2 == 0:
            e = l // 2
            lam_init = 0.8 - 0.6 * math.exp(-0.3 * l)
            p = even_proj(xs, norm_mix[l], bf(even_w_in[e]), even_conv[e], bm=bm, seq=seq)
            qcol = conv_ch // hd_diff
            o = diff_attn(
                p, slopes, even_lambda[e],
                jnp.tile(even_q_gain[e], 2).reshape(1, hd_diff) * (DIFF_HEAD_DIM ** -0.5 * LOG2E),
                jnp.tile(even_k_gain[e], 2).reshape(1, hd_diff),
                even_subln[e].reshape(1, hd_diff),
                batch=batch, seq=seq, bq=bq, lam_init=lam_init,
                q_col=qcol, k_col=qcol + DIFF_HEADS, v_col=qcol + 2 * DIFF_HEADS)
            xs = resid_matmul(xs, [(p, 0, conv_ch), (o, 0, o.shape[1])], bf(even_w_out[e]), bm=bm)
        else:
            k = l // 2
            qkv = norm_proj(xs, norm_mix[l], bf(odd_w_qkv[k]), bm=bm)
            o = sb_attn(qkv, tri, batch=batch, seq=seq, bq=bq)
            xs = resid_matmul(xs, [(o, 0, d)], bf(odd_w_out[k]), bm=bm)

        q = norm_proj(xs, norm_xattn[l], bf(x_w_q[l]), bm=bm)
        kv = norm_proj(mems, norm_mem[l], bf(x_w_kv[l]), bm=min(512, batch * mem_len))
        xs = xattn_out(xs, q, x_q_gain[l].reshape(1, x_hd) * (x_hd ** -0.5),
                       x_k_gain[l].reshape(1, x_hd), kv, bf(x_w_out[l]),
                       bm=bm, seq=seq, mem_len=mem_len)
        xs = ffn(xs, norm_ffn[l], bf(ffn_w_up[l]), ffn_conv[l], bf(ffn_w_down[l]),
                 bm=bm, seq=seq, cf=256)
    return xs.reshape(batch, seq, d)
```

```python
import functools
import math

import jax
import jax.numpy as jnp
from jax import lax
from jax.experimental import pallas as pl
from jax.experimental.pallas import tpu as pltpu

F32 = jnp.float32
BF16 = jnp.bfloat16

EPS = 1e-6
CONV_W = 3
LANES = 128
SUBLANES = 8
BF16_SUBLANES = 16
VMEM_LIMIT = 56 * 1024 * 1024

DIFF_HEADS = 4
DIFF_HEAD_DIM = 64
SB_HEADS = 16
SB_HEAD_DIM = 64
X_HEADS = 4
LOG2E = 1.4426950408889634
MASKED = -1e30


def _cparams(n_axes, flags=None):
    return pltpu.CompilerParams(
        dimension_semantics=("arbitrary",) * n_axes,
        vmem_limit_bytes=VMEM_LIMIT,
        flags=flags,
    )


def _rms(x, g):
    ms = jnp.mean(x * x, axis=-1, keepdims=True)
    return x * lax.rsqrt(ms + EPS) * g


def _dot(a, b):
    return jnp.dot(a, b, preferred_element_type=F32)


def _dot_nt(a, b):
    return lax.dot_general(a, b, (((1,), (1,)), ((), ())), preferred_element_type=F32)


def _causal_conv3(y, prev, w):
    ye = jnp.concatenate([prev, y], axis=0)
    y1 = pltpu.roll(ye, 1, 0)[SUBLANES:]
    y2 = pltpu.roll(ye, 2, 0)[SUBLANES:]
    return w[2:3] * y + w[1:2] * y1 + w[0:1] * y2


def _norm_proj_kernel(x_ref, g_ref, w_ref, o_ref, *, cw):
    h = _rms(x_ref[...], g_ref[...]).astype(BF16)
    for c in range(0, w_ref.shape[1], cw):
        o_ref[:, c:c + cw] = _dot(h, w_ref[:, c:c + cw]).astype(o_ref.dtype)


def norm_proj(x, g, w, *, bm, cw=512):
    m, d = x.shape
    n = w.shape[1]
    return pl.pallas_call(
        functools.partial(_norm_proj_kernel, cw=cw),
        grid=(m // bm,),
        in_specs=[
            pl.BlockSpec((bm, d), lambda i: (i, 0)),
            pl.BlockSpec((1, d), lambda i: (0, 0)),
            pl.BlockSpec((d, n), lambda i: (0, 0)),
        ],
        out_specs=pl.BlockSpec((bm, n), lambda i: (i, 0)),
        out_shape=jax.ShapeDtypeStruct((m, n), BF16),
        compiler_params=_cparams(1),
        name="norm_proj",
    )(x, g.reshape(1, d), w)


def _even_proj_kernel(x_ref, g_ref, w_ref, cw_ref, o_ref, carry_ref, *, blocks_per_seq, cc):
    i = pl.program_id(0)
    h = _rms(x_ref[...], g_ref[...]).astype(BF16)
    a_b = _dot(h, w_ref[:, 0:cc])
    u = _dot(h, w_ref[:, cc:2 * cc]) * _dot(h, w_ref[:, 2 * cc:3 * cc])
    first = (i % blocks_per_seq) == 0
    prev = jnp.where(first, 0.0, carry_ref[...])
    o_ref[:, 0:cc] = (a_b * _causal_conv3(u, prev, cw_ref[...])).astype(o_ref.dtype)
    carry_ref[...] = u[u.shape[0] - SUBLANES:]
    for c in range(3 * cc, w_ref.shape[1], cc):
        o_ref[:, c - 2 * cc:c - cc] = _dot(h, w_ref[:, c:c + cc]).astype(o_ref.dtype)


def even_proj(x, g, w, conv_w, *, bm, seq):
    m, d = x.shape
    n = w.shape[1]
    cc = conv_w.shape[1]
    n_out = n - 2 * cc
    return pl.pallas_call(
        functools.partial(_even_proj_kernel, blocks_per_seq=seq // bm, cc=cc),
        grid=(m // bm,),
        in_specs=[
            pl.BlockSpec((bm, d), lambda i: (i, 0)),
            pl.BlockSpec((1, d), lambda i: (0, 0)),
            pl.BlockSpec((d, n), lambda i: (0, 0)),
            pl.BlockSpec((CONV_W, cc), lambda i: (0, 0)),
        ],
        out_specs=pl.BlockSpec((bm, n_out), lambda i: (i, 0)),
        out_shape=jax.ShapeDtypeStruct((m, n_out), BF16),
        scratch_shapes=[pltpu.VMEM((SUBLANES, cc), F32)],
        compiler_params=_cparams(1),
        name="even_proj",
    )(x, g.reshape(1, d), w, conv_w)


def _resid_matmul_kernel(*refs, n_lhs):
    x_ref = refs[0]
    lhs = refs[1:1 + n_lhs]
    w_ref = refs[1 + n_lhs]
    o_ref = refs[2 + n_lhs]
    acc = x_ref[...]
    k0 = 0
    for a_ref in lhs:
        k = a_ref.shape[1]
        acc = acc + _dot(a_ref[...], w_ref[k0:k0 + k, :])
        k0 += k
    o_ref[...] = acc


def resid_matmul(x, lhs_list, w, *, bm):
    m, d = x.shape
    in_specs = [pl.BlockSpec((bm, d), lambda i: (i, 0))]
    args = [x]
    for arr, cb, width in lhs_list:
        in_specs.append(pl.BlockSpec((bm, width), lambda i, cb=cb: (i, cb)))
        args.append(arr)
    in_specs.append(pl.BlockSpec(w.shape, lambda i: (0, 0)))
    args.append(w)
    return pl.pallas_call(
        functools.partial(_resid_matmul_kernel, n_lhs=len(lhs_list)),
        grid=(m // bm,),
        in_specs=in_specs,
        out_specs=pl.BlockSpec((bm, d), lambda i: (i, 0)),
        out_shape=jax.ShapeDtypeStruct((m, d), F32),
        compiler_params=_cparams(1),
        name="resid_matmul",
    )(*args)


def _half_rms(y, gain):
    lane = lax.broadcasted_iota(jnp.int32, (1, LANES), 1)
    lo = lane < DIFF_HEAD_DIM
    sq = y * y
    s_lo = jnp.sum(jnp.where(lo, sq, 0.0), axis=-1, keepdims=True)
    s_hi = jnp.sum(jnp.where(lo, 0.0, sq), axis=-1, keepdims=True)
    ms = jnp.where(lo, s_lo, s_hi) * (1.0 / DIFF_HEAD_DIM)
    return y * lax.rsqrt(ms + EPS) * gain


def _diff_attn_kernel(slopes_ref, lam_ref, qg_ref, kg_ref, sub_ref, q_ref, k_ref, v_ref,
                      o_ref, kn_ref, ve_ref, acc_ref, *, bq, lam_init, norm_rows):
    g = pl.program_id(1)
    i = pl.program_id(2)
    seq = k_ref.shape[0]
    n_heads = q_ref.shape[1] // LANES
    heads = tuple(range(n_heads))
    hcols = [slice(h * LANES, (h + 1) * LANES) for h in heads]

    @pl.when(i == 0)
    def _():
        def body(r, c):
            rows = pl.ds(pl.multiple_of(r * norm_rows, norm_rows), norm_rows)
            for h in heads:
                kn_ref[rows, hcols[h]] = _half_rms(
                    k_ref[rows, hcols[h]].astype(F32), kg_ref[...]).astype(BF16)
                ve_ref[h, rows, :] = jnp.concatenate(
                    [v_ref[rows, hcols[h]], jnp.ones((norm_rows, LANES), BF16)], axis=1)
            return c
        lax.fori_loop(0, seq // norm_rows, body, 0)

    lane = lax.broadcasted_iota(jnp.int32, (1, LANES), 1)
    lo = lane < DIFF_HEAD_DIM
    row = lax.broadcasted_iota(jnp.int32, (bq, bq), 0)
    col = lax.broadcasted_iota(jnp.int32, (bq, bq), 1)
    dist = (row - col).astype(F32)
    slopes = [slopes_ref[g * n_heads + h] * LOG2E for h in heads]
    qhs = []
    for h in heads:
        qn = _half_rms(q_ref[:, hcols[h]].astype(F32), qg_ref[...])
        qhs += [jnp.where(lo, qn, 0.0).astype(BF16), jnp.where(lo, 0.0, qn).astype(BF16)]
    maps = tuple(range(2 * n_heads))
    biases = [(-s) * dist for s in slopes]
    diag_biases = [jnp.where(row >= col, b, -jnp.inf) for b in biases]

    def tile(j, diagonal, ms):
        rows = pl.ds(pl.multiple_of(j * bq, bq), bq)
        ss = [_dot_nt(qhs[n], kn_ref[rows, hcols[n // 2]])
              + (diag_biases if diagonal else biases)[n // 2] for n in maps]
        ps, new_ms, alphas = [], [], []
        for n in maps:
            nb = (-slopes[n // 2]) * ((i - j) * bq).astype(F32)
            m_new = jnp.maximum(ms[n], jnp.max(ss[n], axis=-1, keepdims=True) + nb)
            alphas.append(jnp.exp2(ms[n] - m_new))
            ps.append(jnp.exp2((ss[n] - (m_new - nb)).astype(BF16)))
            new_ms.append(m_new)
        for n in maps:
            pv = _dot(ps[n], ve_ref[n // 2, rows, :])
            acc_ref[n] = pv if diagonal else alphas[n] * acc_ref[n] + pv
        return tuple(new_ms)

    ms = tile(i, True, (jnp.full((bq, 1), -jnp.inf, F32),) * len(maps))
    lax.fori_loop(0, i, lambda j, ms: tile(j, False, ms), ms)
    accs = [acc_ref[n] for n in maps]

    lf = lam_ref[...]
    lam = (jnp.exp(jnp.sum(lf[0:1] * lf[1:2], axis=-1, keepdims=True))
           - jnp.exp(jnp.sum(lf[2:3] * lf[3:4], axis=-1, keepdims=True)) + lam_init)
    for h in heads:
        a1, a2 = accs[2 * h], accs[2 * h + 1]
        o = a1[:, :LANES] / a1[:, LANES:] - lam * (a2[:, :LANES] / a2[:, LANES:])
        o_ref[:, hcols[h]] = (_rms(o, sub_ref[...]) * (1.0 - lam_init)).astype(o_ref.dtype)


def diff_attn(p, slopes, lam_vecs, q_gain2, k_gain2, subln, *, batch, seq, bq, lam_init,
              q_col, k_col, v_col, heads_per_step):
    nq = seq // bq
    hd = 2 * DIFF_HEAD_DIM
    width = heads_per_step * hd
    groups = DIFF_HEADS // heads_per_step
    return pl.pallas_call(
        functools.partial(_diff_attn_kernel, bq=bq, lam_init=lam_init, norm_rows=min(seq, 512)),
        grid=(batch, groups, nq),
        in_specs=[
            pl.BlockSpec(memory_space=pltpu.SMEM),
            pl.BlockSpec((4, DIFF_HEAD_DIM), lambda b, g, i: (0, 0)),
            pl.BlockSpec((1, hd), lambda b, g, i: (0, 0)),
            pl.BlockSpec((1, hd), lambda b, g, i: (0, 0)),
            pl.BlockSpec((1, hd), lambda b, g, i: (0, 0)),
            pl.BlockSpec((bq, width), lambda b, g, i: (b * nq + i, q_col // heads_per_step + g)),
            pl.BlockSpec((seq, width), lambda b, g, i: (b, k_col // heads_per_step + g)),
            pl.BlockSpec((seq, width), lambda b, g, i: (b, v_col // heads_per_step + g)),
        ],
        out_specs=pl.BlockSpec((bq, width), lambda b, g, i: (b * nq + i, g)),
        out_shape=jax.ShapeDtypeStruct((batch * seq, DIFF_HEADS * hd), BF16),
        scratch_shapes=[pltpu.VMEM((seq, width), BF16),
                        pltpu.VMEM((heads_per_step, seq, hd + LANES), BF16),
                        pltpu.VMEM((2 * heads_per_step, bq, hd + LANES), F32)],
        compiler_params=_cparams(3),
        name="diff_attn",
    )(slopes, lam_vecs, q_gain2, k_gain2, subln, p, p, p)


def _sb_attn_kernel(tri_ref, q_ref, k_ref, v_ref, o_ref, vt_ref, *, bq, scale):
    bk = bq // 2
    i = pl.program_id(2)
    n_slabs = vt_ref.shape[0]

    @pl.when(i == 0)
    def _():
        for j in range(n_slabs):
            vt_ref[j] = v_ref[j * bq:(j + 1) * bq, :].astype(F32).T.astype(BF16)

    lane = lax.broadcasted_iota(jnp.int32, (1, LANES), 1)
    lo = lane < SB_HEAD_DIM
    key = lax.broadcasted_iota(jnp.int32, (bq, bq), 0)
    qry = lax.broadcasted_iota(jnp.int32, (bq, bq), 1)
    mask_bias = jnp.where(key < qry, 0.0, MASKED)
    tri = tri_ref[...]

    n_heads = q_ref.shape[1] // SB_HEAD_DIM
    heads = tuple(range(n_heads))
    qs = []
    for h in heads:
        q = q_ref[:, (h // 2) * LANES:(h // 2 + 1) * LANES] * scale
        qs.append(jnp.where(lo if h % 2 == 0 else jnp.logical_not(lo), q, jnp.zeros_like(q)))

    def slab(j, bias, accs, runs):
        rows = pl.ds(pl.multiple_of(j * bq, bq), bq)
        zs = [_dot_nt(k_ref[rows, (h // 2) * LANES:(h // 2 + 1) * LANES], qs[h]) for h in heads]
        lbs, l1ms = [], []
        for h in heads:
            zb = (zs[h] if bias is None else zs[h] + bias).astype(BF16)
            lp = jnp.log(1.0 + jnp.exp(-jnp.abs(zb)))
            lb = jnp.minimum(zb, 0.0) - lp
            lbs.append(lb)
            l1ms.append(lb - zb)
        ths = [_dot(tri, l1ms[h][bk:, :]) for h in heads]
        tls = [_dot(tri, l1ms[h][:bk, :]) for h in heads]
        a_s, new_runs = [], []
        for h in heads:
            run_lo = runs[h] + ths[h][bk:bk + 1, :]
            x = jnp.concatenate([tls[h][:bk, :] + run_lo, ths[h][:bk, :] + runs[h]], axis=0)
            a_s.append(jnp.exp(x.astype(BF16) + lbs[h]))
            new_runs.append(run_lo + tls[h][bk:bk + 1, :])
        new_accs = [accs[h] + _dot(vt_ref[j, h * SB_HEAD_DIM:(h + 1) * SB_HEAD_DIM, :], a_s[h])
                    for h in heads]
        return tuple(new_accs), tuple(new_runs)

    zero_acc = (jnp.zeros((SB_HEAD_DIM, bq), F32),) * n_heads
    zero_run = (jnp.zeros((1, bq), F32),) * n_heads
    state = slab(i, mask_bias, zero_acc, zero_run)

    def body(t, carry):
        return slab(i - 1 - t, None, *carry)

    accs, _ = lax.fori_loop(0, i, body, state)
    o_ref[...] = jnp.concatenate(accs, axis=0).T.astype(o_ref.dtype)


def sb_attn(qkv, tri, *, batch, seq, bq, heads_per_step):
    nq = seq // bq
    width = heads_per_step * SB_HEAD_DIM
    groups = SB_HEADS // heads_per_step
    return pl.pallas_call(
        functools.partial(_sb_attn_kernel, bq=bq, scale=SB_HEAD_DIM ** -0.5),
        grid=(batch, groups, nq),
        in_specs=[
            pl.BlockSpec(tri.shape, lambda b, g, i: (0, 0)),
            pl.BlockSpec((bq, width), lambda b, g, i: (b * nq + i, g)),
            pl.BlockSpec((seq, width), lambda b, g, i: (b, groups + g)),
            pl.BlockSpec((seq, width), lambda b, g, i: (b, 2 * groups + g)),
        ],
        out_specs=pl.BlockSpec((bq, width), lambda b, g, i: (b * nq + i, g)),
        out_shape=jax.ShapeDtypeStruct((batch * seq, groups * width), BF16),
        scratch_shapes=[pltpu.VMEM((nq, width, bq), BF16)],
        compiler_params=_cparams(3),
        name="sb_attn",
    )(tri, qkv, qkv, qkv)


def _xattn_kernel(x_ref, q_ref, qg_ref, kg_ref, k_ref, v_ref, wo_ref, o_ref, kn_ref, oc_ref,
                  *, blocks_per_seq, hd):
    i = pl.program_id(0)

    @pl.when(i % blocks_per_seq == 0)
    def _():
        for h in range(X_HEADS):
            cols = slice(h * hd, (h + 1) * hd)
            kn_ref[:, cols] = _rms(k_ref[:, cols].astype(F32), kg_ref[...]).astype(BF16)

    for h in range(X_HEADS):
        cols = slice(h * hd, (h + 1) * hd)
        qh = _rms(q_ref[:, cols].astype(F32), qg_ref[...]).astype(BF16)
        s = _dot_nt(qh, kn_ref[:, cols])
        p = jnp.exp(s - jnp.max(s, axis=-1, keepdims=True))
        l = jnp.sum(p, axis=-1, keepdims=True)
        oc_ref[:, cols] = (_dot(p.astype(BF16), v_ref[:, cols]) / l).astype(BF16)
    o_ref[...] = x_ref[...] + _dot(oc_ref[...], wo_ref[...])


def xattn_out(x, q, q_gain, k_gain, kv, w_o, *, bm, seq, mem_len):
    m, d = x.shape
    hd = d // X_HEADS
    blocks_per_seq = seq // bm
    return pl.pallas_call(
        functools.partial(_xattn_kernel, blocks_per_seq=blocks_per_seq, hd=hd),
        grid=(m // bm,),
        in_specs=[
            pl.BlockSpec((bm, d), lambda i: (i, 0)),
            pl.BlockSpec((bm, d), lambda i: (i, 0)),
            pl.BlockSpec((1, hd), lambda i: (0, 0)),
            pl.BlockSpec((1, hd), lambda i: (0, 0)),
            pl.BlockSpec((mem_len, d), lambda i: (i // blocks_per_seq, 0)),
            pl.BlockSpec((mem_len, d), lambda i: (i // blocks_per_seq, 1)),
            pl.BlockSpec((d, d), lambda i: (0, 0)),
        ],
        out_specs=pl.BlockSpec((bm, d), lambda i: (i, 0)),
        out_shape=jax.ShapeDtypeStruct((m, d), F32),
        scratch_shapes=[pltpu.VMEM((mem_len, d), BF16), pltpu.VMEM((bm, d), BF16)],
        compiler_params=_cparams(1),
        name="xattn_out",
    )(x, q, q_gain, k_gain, kv, kv, w_o)


def _ffn_kernel(x_ref, g_ref, wu_ref, cw_ref, wd_ref, o_ref, pg_ref, pv_ref, *, blocks_per_seq, cf):
    i = pl.program_id(0)
    bm = x_ref.shape[0]
    d_ff = wd_ref.shape[0]
    first = (i % blocks_per_seq) == 0
    x = x_ref[...]
    h = _rms(x, g_ref[...]).astype(BF16)

    def branch(col, prev_ref, c):
        y = _dot(h, wu_ref[:, col:col + cf])
        prev = jnp.where(first, 0.0, prev_ref[c])
        prev_ref[c] = y[bm - SUBLANES:]
        return _causal_conv3(y, prev, cw_ref[:, col:col + cf])

    acc = x
    for c in range(d_ff // cf):
        gate = branch(c * cf, pg_ref, c)
        val = branch(d_ff + c * cf, pv_ref, c)
        act = gate * (1.0 / (1.0 + jnp.exp(-gate))) * val
        acc = acc + _dot(act.astype(BF16), wd_ref[c * cf:(c + 1) * cf, :])
    o_ref[...] = acc


def ffn(x, g, w_up, w_conv, w_down, *, bm, seq, cf):
    m, d = x.shape
    d_ff = w_down.shape[0]
    nc = d_ff // cf
    resident = dict(pipeline_mode=pl.Buffered(1))
    return pl.pallas_call(
        functools.partial(_ffn_kernel, blocks_per_seq=seq // bm, cf=cf),
        grid=(m // bm,),
        in_specs=[
            pl.BlockSpec((bm, d), lambda i: (i, 0)),
            pl.BlockSpec((1, d), lambda i: (0, 0)),
            pl.BlockSpec((d, 2 * d_ff), lambda i: (0, 0), **resident),
            pl.BlockSpec((CONV_W, 2 * d_ff), lambda i: (0, 0), **resident),
            pl.BlockSpec((d_ff, d), lambda i: (0, 0), **resident),
        ],
        out_specs=pl.BlockSpec((bm, d), lambda i: (i, 0)),
        out_shape=jax.ShapeDtypeStruct((m, d), F32),
        scratch_shapes=[
            pltpu.VMEM((nc, SUBLANES, cf), F32),
            pltpu.VMEM((nc, SUBLANES, cf), F32),
        ],
        compiler_params=_cparams(1),
        name="ffn",
    )(x, g.reshape(1, d), w_up, w_conv, w_down)


def kernel(x, mem, norm_mix, norm_xattn, norm_mem, norm_ffn, even_w_in, even_conv, even_q_gain,
           even_k_gain, even_lambda, even_subln, even_w_out, odd_w_qkv, odd_w_out, x_w_q, x_w_kv,
           x_q_gain, x_k_gain, x_w_out, ffn_w_up, ffn_conv, ffn_w_down):
    batch, seq, d = x.shape
    mem_len = mem.shape[1]
    depth = norm_mix.shape[0]
    n = batch * seq
    bm = min(512, seq)
    bq = min(512, seq)
    bk_sb = bq // 2

    xs = x.reshape(n, d)
    mems = mem.reshape(batch * mem_len, d)
    bf = lambda a: a.astype(BF16)

    slopes = jnp.asarray([2.0 ** (-8.0 * (h + 1) / DIFF_HEADS) for h in range(DIFF_HEADS)], F32)
    tri = jnp.concatenate(
        [(lax.broadcasted_iota(jnp.int32, (bk_sb, bk_sb), 1)
          > lax.broadcasted_iota(jnp.int32, (bk_sb, bk_sb), 0)).astype(BF16),
         jnp.ones((BF16_SUBLANES, bk_sb), BF16)], axis=0)
    conv_ch = even_conv.shape[2]
    hd_diff = 2 * DIFF_HEAD_DIM
    x_hd = d // X_HEADS

    for l in range(depth):
        if l % 2 == 0:
            e = l // 2
            lam_init = 0.8 - 0.6 * math.exp(-0.3 * l)
            p = even_proj(xs, norm_mix[l], bf(even_w_in[e]), even_conv[e], bm=bm, seq=seq)
            qcol = conv_ch // hd_diff
            o = diff_attn(
                p, slopes, even_lambda[e],
                jnp.tile(even_q_gain[e], 2).reshape(1, hd_diff) * (DIFF_HEAD_DIM ** -0.5 * LOG2E),
                jnp.tile(even_k_gain[e], 2).reshape(1, hd_diff),
                even_subln[e].reshape(1, hd_diff),
                batch=batch, seq=seq, bq=bq, lam_init=lam_init,
                q_col=qcol, k_col=qcol + DIFF_HEADS, v_col=qcol + 2 * DIFF_HEADS,
                heads_per_step=2)
            xs = resid_matmul(xs, [(p, 0, conv_ch), (o, 0, o.shape[1])], bf(even_w_out[e]), bm=bm)
        else:
            k = l // 2
            qkv = norm_proj(xs, norm_mix[l], bf(odd_w_qkv[k]), bm=bm)
            o = sb_attn(qkv, tri, batch=batch, seq=seq, bq=bq, heads_per_step=4)
            xs = resid_matmul(xs, [(o, 0, d)], bf(odd_w_out[k]), bm=bm)

        q = norm_proj(xs, norm_xattn[l], bf(x_w_q[l]), bm=bm)
        kv = norm_proj(mems, norm_mem[l], bf(x_w_kv[l]), bm=min(512, batch * mem_len))
        xs = xattn_out(xs, q, x_q_gain[l].reshape(1, x_hd) * (x_hd ** -0.5),
                       x_k_gain[l].reshape(1, x_hd), kv, bf(x_w_out[l]),
                       bm=bm, seq=seq, mem_len=mem_len)
        xs = ffn(xs, norm_ffn[l], bf(ffn_w_up[l]), ffn_conv[l], bf(ffn_w_down[l]),
                 bm=bm, seq=seq, cf=256)
    return xs.reshape(batch, seq, d)
```

```python
import functools
import math

import jax
import jax.numpy as jnp
from jax import lax
from jax.experimental import pallas as pl
from jax.experimental.pallas import tpu as pltpu

F32 = jnp.float32
BF16 = jnp.bfloat16

EPS = 1e-6
CONV_W = 3
LANES = 128
SUBLANES = 8
BF16_SUBLANES = 16
VMEM_LIMIT = 56 * 1024 * 1024

DIFF_HEADS = 4
DIFF_HEAD_DIM = 64
SB_HEADS = 16
SB_HEAD_DIM = 64
X_HEADS = 4
LOG2E = 1.4426950408889634
MASKED = -1e30
SB_DEAD_BELOW = -120.0


def _cparams(n_axes, flags=None):
    return pltpu.CompilerParams(
        dimension_semantics=("arbitrary",) * n_axes,
        vmem_limit_bytes=VMEM_LIMIT,
        flags=flags,
    )


def _rms(x, g):
    ms = jnp.mean(x * x, axis=-1, keepdims=True)
    return x * lax.rsqrt(ms + EPS) * g


def _dot(a, b):
    return jnp.dot(a, b, preferred_element_type=F32)


def _dot_nt(a, b):
    return lax.dot_general(a, b, (((1,), (1,)), ((), ())), preferred_element_type=F32)


def _causal_conv3(y, prev, w):
    ye = jnp.concatenate([prev, y], axis=0)
    y1 = pltpu.roll(ye, 1, 0)[SUBLANES:]
    y2 = pltpu.roll(ye, 2, 0)[SUBLANES:]
    return w[2:3] * y + w[1:2] * y1 + w[0:1] * y2


def _norm_proj_kernel(x_ref, g_ref, w_ref, o_ref, *, cw):
    h = _rms(x_ref[...], g_ref[...]).astype(BF16)
    for c in range(0, w_ref.shape[1], cw):
        o_ref[:, c:c + cw] = _dot(h, w_ref[:, c:c + cw]).astype(o_ref.dtype)


def norm_proj(x, g, w, *, bm, cw=512):
    m, d = x.shape
    n = w.shape[1]
    return pl.pallas_call(
        functools.partial(_norm_proj_kernel, cw=cw),
        grid=(m // bm,),
        in_specs=[
            pl.BlockSpec((bm, d), lambda i: (i, 0)),
            pl.BlockSpec((1, d), lambda i: (0, 0)),
            pl.BlockSpec((d, n), lambda i: (0, 0)),
        ],
        out_specs=pl.BlockSpec((bm, n), lambda i: (i, 0)),
        out_shape=jax.ShapeDtypeStruct((m, n), BF16),
        compiler_params=_cparams(1),
        name="norm_proj",
    )(x, g.reshape(1, d), w)


def _even_proj_kernel(x_ref, g_ref, w_ref, cw_ref, o_ref, carry_ref, *, blocks_per_seq, cc):
    i = pl.program_id(0)
    h = _rms(x_ref[...], g_ref[...]).astype(BF16)
    a_b = _dot(h, w_ref[:, 0:cc])
    u = _dot(h, w_ref[:, cc:2 * cc]) * _dot(h, w_ref[:, 2 * cc:3 * cc])
    first = (i % blocks_per_seq) == 0
    prev = jnp.where(first, 0.0, carry_ref[...])
    o_ref[:, 0:cc] = (a_b * _causal_conv3(u, prev, cw_ref[...])).astype(o_ref.dtype)
    carry_ref[...] = u[u.shape[0] - SUBLANES:]
    for c in range(3 * cc, w_ref.shape[1], cc):
        o_ref[:, c - 2 * cc:c - cc] = _dot(h, w_ref[:, c:c + cc]).astype(o_ref.dtype)


def even_proj(x, g, w, conv_w, *, bm, seq):
    m, d = x.shape
    n = w.shape[1]
    cc = conv_w.shape[1]
    n_out = n - 2 * cc
    return pl.pallas_call(
        functools.partial(_even_proj_kernel, blocks_per_seq=seq // bm, cc=cc),
        grid=(m // bm,),
        in_specs=[
            pl.BlockSpec((bm, d), lambda i: (i, 0)),
            pl.BlockSpec((1, d), lambda i: (0, 0)),
            pl.BlockSpec((d, n), lambda i: (0, 0)),
            pl.BlockSpec((CONV_W, cc), lambda i: (0, 0)),
        ],
        out_specs=pl.BlockSpec((bm, n_out), lambda i: (i, 0)),
        out_shape=jax.ShapeDtypeStruct((m, n_out), BF16),
        scratch_shapes=[pltpu.VMEM((SUBLANES, cc), F32)],
        compiler_params=_cparams(1),
        name="even_proj",
    )(x, g.reshape(1, d), w, conv_w)


def _resid_matmul_kernel(*refs, n_lhs):
    x_ref = refs[0]
    lhs = refs[1:1 + n_lhs]
    w_ref = refs[1 + n_lhs]
    o_ref = refs[2 + n_lhs]
    acc = x_ref[...]
    k0 = 0
    for a_ref in lhs:
        k = a_ref.shape[1]
        acc = acc + _dot(a_ref[...], w_ref[k0:k0 + k, :])
        k0 += k
    o_ref[...] = acc


def resid_matmul(x, lhs_list, w, *, bm):
    m, d = x.shape
    in_specs = [pl.BlockSpec((bm, d), lambda i: (i, 0))]
    args = [x]
    for arr, cb, width in lhs_list:
        in_specs.append(pl.BlockSpec((bm, width), lambda i, cb=cb: (i, cb)))
        args.append(arr)
    in_specs.append(pl.BlockSpec(w.shape, lambda i: (0, 0)))
    args.append(w)
    return pl.pallas_call(
        functools.partial(_resid_matmul_kernel, n_lhs=len(lhs_list)),
        grid=(m // bm,),
        in_specs=in_specs,
        out_specs=pl.BlockSpec((bm, d), lambda i: (i, 0)),
        out_shape=jax.ShapeDtypeStruct((m, d), F32),
        compiler_params=_cparams(1),
        name="resid_matmul",
    )(*args)


def _half_rms(y, gain):
    lane = lax.broadcasted_iota(jnp.int32, (1, LANES), 1)
    lo = lane < DIFF_HEAD_DIM
    sq = y * y
    s_lo = jnp.sum(jnp.where(lo, sq, 0.0), axis=-1, keepdims=True)
    s_hi = jnp.sum(jnp.where(lo, 0.0, sq), axis=-1, keepdims=True)
    ms = jnp.where(lo, s_lo, s_hi) * (1.0 / DIFF_HEAD_DIM)
    return y * lax.rsqrt(ms + EPS) * gain


def _diff_attn_kernel(slopes_ref, lam_ref, qg_ref, kg_ref, sub_ref, q_ref, k_ref, v_ref,
                      o_ref, kn_ref, ve_ref, acc_ref, *, bq, lam_init, norm_rows):
    g = pl.program_id(1)
    i = pl.program_id(2)
    seq = k_ref.shape[0]
    n_heads = q_ref.shape[1] // LANES
    heads = tuple(range(n_heads))
    hcols = [slice(h * LANES, (h + 1) * LANES) for h in heads]

    @pl.when(i == 0)
    def _():
        def body(r, c):
            rows = pl.ds(pl.multiple_of(r * norm_rows, norm_rows), norm_rows)
            for h in heads:
                kn_ref[rows, hcols[h]] = _half_rms(
                    k_ref[rows, hcols[h]].astype(F32), kg_ref[...]).astype(BF16)
                ve_ref[h, rows, :] = jnp.concatenate(
                    [v_ref[rows, hcols[h]], jnp.ones((norm_rows, LANES), BF16)], axis=1)
            return c
        lax.fori_loop(0, seq // norm_rows, body, 0)

    lane = lax.broadcasted_iota(jnp.int32, (1, LANES), 1)
    lo = lane < DIFF_HEAD_DIM
    row = lax.broadcasted_iota(jnp.int32, (bq, bq), 0)
    col = lax.broadcasted_iota(jnp.int32, (bq, bq), 1)
    dist = (row - col).astype(F32)
    slopes = [slopes_ref[g * n_heads + h] * LOG2E for h in heads]
    qhs = []
    for h in heads:
        qn = _half_rms(q_ref[:, hcols[h]].astype(F32), qg_ref[...])
        qhs += [jnp.where(lo, qn, 0.0).astype(BF16), jnp.where(lo, 0.0, qn).astype(BF16)]
    maps = tuple(range(2 * n_heads))
    biases = [(-s) * dist for s in slopes]
    diag_biases = [jnp.where(row >= col, b, -jnp.inf) for b in biases]

    def tile(j, diagonal, ms):
        rows = pl.ds(pl.multiple_of(j * bq, bq), bq)
        ss = [_dot_nt(qhs[n], kn_ref[rows, hcols[n // 2]])
              + (diag_biases if diagonal else biases)[n // 2] for n in maps]
        ps, new_ms, alphas = [], [], []
        for n in maps:
            nb = (-slopes[n // 2]) * ((i - j) * bq).astype(F32)
            m_new = jnp.maximum(ms[n], jnp.max(ss[n], axis=-1, keepdims=True) + nb)
            alphas.append(jnp.exp2(ms[n] - m_new))
            ps.append(jnp.exp2((ss[n] - (m_new - nb)).astype(BF16)))
            new_ms.append(m_new)
        for n in maps:
            pv = _dot(ps[n], ve_ref[n // 2, rows, :])
            acc_ref[n] = pv if diagonal else alphas[n] * acc_ref[n] + pv
        return tuple(new_ms)

    ms = tile(i, True, (jnp.full((bq, 1), -jnp.inf, F32),) * len(maps))
    lax.fori_loop(0, i, lambda j, ms: tile(j, False, ms), ms)
    accs = [acc_ref[n] for n in maps]

    lf = lam_ref[...]
    lam = (jnp.exp(jnp.sum(lf[0:1] * lf[1:2], axis=-1, keepdims=True))
           - jnp.exp(jnp.sum(lf[2:3] * lf[3:4], axis=-1, keepdims=True)) + lam_init)
    for h in heads:
        a1, a2 = accs[2 * h], accs[2 * h + 1]
        o = a1[:, :LANES] / a1[:, LANES:] - lam * (a2[:, :LANES] / a2[:, LANES:])
        o_ref[:, hcols[h]] = (_rms(o, sub_ref[...]) * (1.0 - lam_init)).astype(o_ref.dtype)


def diff_attn(p, slopes, lam_vecs, q_gain2, k_gain2, subln, *, batch, seq, bq, lam_init,
              q_col, k_col, v_col, heads_per_step):
    nq = seq // bq
    hd = 2 * DIFF_HEAD_DIM
    width = heads_per_step * hd
    groups = DIFF_HEADS // heads_per_step
    return pl.pallas_call(
        functools.partial(_diff_attn_kernel, bq=bq, lam_init=lam_init, norm_rows=min(seq, 512)),
        grid=(batch, groups, nq),
        in_specs=[
            pl.BlockSpec(memory_space=pltpu.SMEM),
            pl.BlockSpec((4, DIFF_HEAD_DIM), lambda b, g, i: (0, 0)),
            pl.BlockSpec((1, hd), lambda b, g, i: (0, 0)),
            pl.BlockSpec((1, hd), lambda b, g, i: (0, 0)),
            pl.BlockSpec((1, hd), lambda b, g, i: (0, 0)),
            pl.BlockSpec((bq, width), lambda b, g, i: (b * nq + i, q_col // heads_per_step + g)),
            pl.BlockSpec((seq, width), lambda b, g, i: (b, k_col // heads_per_step + g)),
            pl.BlockSpec((seq, width), lambda b, g, i: (b, v_col // heads_per_step + g)),
        ],
        out_specs=pl.BlockSpec((bq, width), lambda b, g, i: (b * nq + i, g)),
        out_shape=jax.ShapeDtypeStruct((batch * seq, DIFF_HEADS * hd), BF16),
        scratch_shapes=[pltpu.VMEM((seq, width), BF16),
                        pltpu.VMEM((heads_per_step, seq, hd + LANES), BF16),
                        pltpu.VMEM((2 * heads_per_step, bq, hd + LANES), F32)],
        compiler_params=_cparams(3),
        name="diff_attn",
    )(slopes, lam_vecs, q_gain2, k_gain2, subln, p, p, p)


def _sb_attn_kernel(tri_ref, q_ref, k_ref, v_ref, o_ref, vt_ref, *, bq, scale):
    bk = bq // 2
    i = pl.program_id(2)
    n_slabs = vt_ref.shape[0]

    @pl.when(i == 0)
    def _():
        for j in range(n_slabs):
            vt_ref[j] = v_ref[j * bq:(j + 1) * bq, :].astype(F32).T.astype(BF16)

    lane = lax.broadcasted_iota(jnp.int32, (1, LANES), 1)
    lo = lane < SB_HEAD_DIM
    key = lax.broadcasted_iota(jnp.int32, (bq, bq), 0)
    qry = lax.broadcasted_iota(jnp.int32, (bq, bq), 1)
    mask_bias = jnp.where(key < qry, 0.0, MASKED)
    tri = tri_ref[...]

    n_heads = q_ref.shape[1] // SB_HEAD_DIM
    heads = tuple(range(n_heads))
    qs = []
    for h in heads:
        q = q_ref[:, (h // 2) * LANES:(h // 2 + 1) * LANES] * scale
        qs.append(jnp.where(lo if h % 2 == 0 else jnp.logical_not(lo), q, jnp.zeros_like(q)))

    def slab(j, bias, accs, runs):
        rows = pl.ds(pl.multiple_of(j * bq, bq), bq)
        zs = [_dot_nt(k_ref[rows, (h // 2) * LANES:(h // 2 + 1) * LANES], qs[h]) for h in heads]
        lbs, l1ms = [], []
        for h in heads:
            zb = (zs[h] if bias is None else zs[h] + bias).astype(BF16)
            lp = jnp.log(1.0 + jnp.exp(-jnp.abs(zb)))
            lb = jnp.minimum(zb, 0.0) - lp
            lbs.append(lb)
            l1ms.append(lb - zb)
        ths = [_dot(tri, l1ms[h][bk:, :]) for h in heads]
        tls = [_dot(tri, l1ms[h][:bk, :]) for h in heads]
        a_s, new_runs = [], []
        for h in heads:
            run_lo = runs[h] + ths[h][bk:bk + 1, :]
            x = jnp.concatenate([tls[h][:bk, :] + run_lo, ths[h][:bk, :] + runs[h]], axis=0)
            a_s.append(jnp.exp(x.astype(BF16) + lbs[h]))
            new_runs.append(run_lo + tls[h][bk:bk + 1, :])
        new_accs = [accs[h] + _dot(vt_ref[j, h * SB_HEAD_DIM:(h + 1) * SB_HEAD_DIM, :], a_s[h])
                    for h in heads]
        return tuple(new_accs), tuple(new_runs)

    zero_acc = (jnp.zeros((SB_HEAD_DIM, bq), F32),) * n_heads
    zero_run = (jnp.zeros((1, bq), F32),) * n_heads
    def any_weight_left(runs):
        return jnp.max(functools.reduce(jnp.maximum, runs)) > SB_DEAD_BELOW

    def cond(carry):
        t, live, _, _ = carry
        return jnp.logical_and(t < i, live)

    def body(carry):
        t, _, accs, runs = carry
        accs, runs = slab(i - 1 - t, None, accs, runs)
        return t + 1, any_weight_left(runs), accs, runs

    accs, runs = slab(i, mask_bias, zero_acc, zero_run)
    _, _, accs, _ = lax.while_loop(cond, body, (jnp.int32(0), any_weight_left(runs), accs, runs))
    o_ref[...] = jnp.concatenate(accs, axis=0).T.astype(o_ref.dtype)


def sb_attn(qkv, tri, *, batch, seq, bq, heads_per_step):
    nq = seq // bq
    width = heads_per_step * SB_HEAD_DIM
    groups = SB_HEADS // heads_per_step
    return pl.pallas_call(
        functools.partial(_sb_attn_kernel, bq=bq, scale=SB_HEAD_DIM ** -0.5),
        grid=(batch, groups, nq),
        in_specs=[
            pl.BlockSpec(tri.shape, lambda b, g, i: (0, 0)),
            pl.BlockSpec((bq, width), lambda b, g, i: (b * nq + i, g)),
            pl.BlockSpec((seq, width), lambda b, g, i: (b, groups + g)),
            pl.BlockSpec((seq, width), lambda b, g, i: (b, 2 * groups + g)),
        ],
        out_specs=pl.BlockSpec((bq, width), lambda b, g, i: (b * nq + i, g)),
        out_shape=jax.ShapeDtypeStruct((batch * seq, groups * width), BF16),
        scratch_shapes=[pltpu.VMEM((nq, width, bq), BF16)],
        compiler_params=_cparams(3),
        name="sb_attn",
    )(tri, qkv, qkv, qkv)


def _xattn_kernel(x_ref, q_ref, qg_ref, kg_ref, k_ref, v_ref, wo_ref, o_ref, kn_ref, oc_ref,
                  *, blocks_per_seq, hd):
    i = pl.program_id(0)

    @pl.when(i % blocks_per_seq == 0)
    def _():
        for h in range(X_HEADS):
            cols = slice(h * hd, (h + 1) * hd)
            kn_ref[:, cols] = _rms(k_ref[:, cols].astype(F32), kg_ref[...]).astype(BF16)

    for h in range(X_HEADS):
        cols = slice(h * hd, (h + 1) * hd)
        qh = _rms(q_ref[:, cols].astype(F32), qg_ref[...]).astype(BF16)
        s = _dot_nt(qh, kn_ref[:, cols])
        p = jnp.exp(s - jnp.max(s, axis=-1, keepdims=True))
        l = jnp.sum(p, axis=-1, keepdims=True)
        oc_ref[:, cols] = (_dot(p.astype(BF16), v_ref[:, cols]) / l).astype(BF16)
    o_ref[...] = x_ref[...] + _dot(oc_ref[...], wo_ref[...])


def xattn_out(x, q, q_gain, k_gain, kv, w_o, *, bm, seq, mem_len):
    m, d = x.shape
    hd = d // X_HEADS
    blocks_per_seq = seq // bm
    return pl.pallas_call(
        functools.partial(_xattn_kernel, blocks_per_seq=blocks_per_seq, hd=hd),
        grid=(m // bm,),
        in_specs=[
            pl.BlockSpec((bm, d), lambda i: (i, 0)),
            pl.BlockSpec((bm, d), lambda i: (i, 0)),
            pl.BlockSpec((1, hd), lambda i: (0, 0)),
            pl.BlockSpec((1, hd), lambda i: (0, 0)),
            pl.BlockSpec((mem_len, d), lambda i: (i // blocks_per_seq, 0)),
            pl.BlockSpec((mem_len, d), lambda i: (i // blocks_per_seq, 1)),
            pl.BlockSpec((d, d), lambda i: (0, 0)),
        ],
        out_specs=pl.BlockSpec((bm, d), lambda i: (i, 0)),
        out_shape=jax.ShapeDtypeStruct((m, d), F32),
        scratch_shapes=[pltpu.VMEM((mem_len, d), BF16), pltpu.VMEM((bm, d), BF16)],
        compiler_params=_cparams(1),
        name="xattn_out",
    )(x, q, q_gain, k_gain, kv, kv, w_o)


def _ffn_kernel(x_ref, g_ref, wu_ref, cw_ref, wd_ref, o_ref, pg_ref, pv_ref, *, blocks_per_seq, cf):
    i = pl.program_id(0)
    bm = x_ref.shape[0]
    d_ff = wd_ref.shape[0]
    first = (i % blocks_per_seq) == 0
    x = x_ref[...]
    h = _rms(x, g_ref[...]).astype(BF16)

    def up(c):
        return (_dot(h, wu_ref[:, c * cf:(c + 1) * cf]),
                _dot(h, wu_ref[:, d_ff + c * cf:d_ff + (c + 1) * cf]))

    def conv(y, col, prev_ref, c):
        prev = jnp.where(first, 0.0, prev_ref[c])
        prev_ref[c] = y[bm - SUBLANES:]
        return _causal_conv3(y, prev, cw_ref[:, col:col + cf])

    n_chunks = d_ff // cf
    acc = x
    ys = up(0)
    for c in range(n_chunks):
        ys_next = up(c + 1) if c + 1 < n_chunks else None
        gate = conv(ys[0], c * cf, pg_ref, c)
        val = conv(ys[1], d_ff + c * cf, pv_ref, c)
        act = gate * (1.0 / (1.0 + jnp.exp(-gate))) * val
        acc = acc + _dot(act.astype(BF16), wd_ref[c * cf:(c + 1) * cf, :])
        ys = ys_next
    o_ref[...] = acc


def ffn(x, g, w_up, w_conv, w_down, *, bm, seq, cf):
    m, d = x.shape
    d_ff = w_down.shape[0]
    nc = d_ff // cf
    resident = dict(pipeline_mode=pl.Buffered(1))
    return pl.pallas_call(
        functools.partial(_ffn_kernel, blocks_per_seq=seq // bm, cf=cf),
        grid=(m // bm,),
        in_specs=[
            pl.BlockSpec((bm, d), lambda i: (i, 0)),
            pl.BlockSpec((1, d), lambda i: (0, 0)),
            pl.BlockSpec((d, 2 * d_ff), lambda i: (0, 0), **resident),
            pl.BlockSpec((CONV_W, 2 * d_ff), lambda i: (0, 0), **resident),
            pl.BlockSpec((d_ff, d), lambda i: (0, 0), **resident),
        ],
        out_specs=pl.BlockSpec((bm, d), lambda i: (i, 0)),
        out_shape=jax.ShapeDtypeStruct((m, d), F32),
        scratch_shapes=[
            pltpu.VMEM((nc, SUBLANES, cf), F32),
            pltpu.VMEM((nc, SUBLANES, cf), F32),
        ],
        compiler_params=_cparams(1),
        name="ffn",
    )(x, g.reshape(1, d), w_up, w_conv, w_down)


def kernel(x, mem, norm_mix, norm_xattn, norm_mem, norm_ffn, even_w_in, even_conv, even_q_gain,
           even_k_gain, even_lambda, even_subln, even_w_out, odd_w_qkv, odd_w_out, x_w_q, x_w_kv,
           x_q_gain, x_k_gain, x_w_out, ffn_w_up, ffn_conv, ffn_w_down):
    batch, seq, d = x.shape
    mem_len = mem.shape[1]
    depth = norm_mix.shape[0]
    n = batch * seq
    bm = min(512, seq)
    bq = min(512, seq)
    bk_sb = bq // 2

    xs = x.reshape(n, d)
    mems = mem.reshape(batch * mem_len, d)
    bf = lambda a: a.astype(BF16)

    slopes = jnp.asarray([2.0 ** (-8.0 * (h + 1) / DIFF_HEADS) for h in range(DIFF_HEADS)], F32)
    tri = jnp.concatenate(
        [(lax.broadcasted_iota(jnp.int32, (bk_sb, bk_sb), 1)
          > lax.broadcasted_iota(jnp.int32, (bk_sb, bk_sb), 0)).astype(BF16),
         jnp.ones((BF16_SUBLANES, bk_sb), BF16)], axis=0)
    conv_ch = even_conv.shape[2]
    hd_diff = 2 * DIFF_HEAD_DIM
    x_hd = d // X_HEADS

    for l in range(depth):
        if l % 2 == 0:
            e = l // 2
            lam_init = 0.8 - 0.6 * math.exp(-0.3 * l)
            p = even_proj(xs, norm_mix[l], bf(even_w_in[e]), even_conv[e], bm=bm, seq=seq)
            qcol = conv_ch // hd_diff
            o = diff_attn(
                p, slopes, even_lambda[e],
                jnp.tile(even_q_gain[e], 2).reshape(1, hd_diff) * (DIFF_HEAD_DIM ** -0.5 * LOG2E),
                jnp.tile(even_k_gain[e], 2).reshape(1, hd_diff),
                even_subln[e].reshape(1, hd_diff),
                batch=batch, seq=seq, bq=bq, lam_init=lam_init,
                q_col=qcol, k_col=qcol + DIFF_HEADS, v_col=qcol + 2 * DIFF_HEADS,
                heads_per_step=2)
            xs = resid_matmul(xs, [(p, 0, conv_ch), (o, 0, o.shape[1])], bf(even_w_out[e]), bm=bm)
        else:
            k = l // 2
            qkv = norm_proj(xs, norm_mix[l], bf(odd_w_qkv[k]), bm=bm)
            o = sb_attn(qkv, tri, batch=batch, seq=seq, bq=bq, heads_per_step=4)
            xs = resid_matmul(xs, [(o, 0, d)], bf(odd_w_out[k]), bm=bm)

        q = norm_proj(xs, norm_xattn[l], bf(x_w_q[l]), bm=bm)
        kv = norm_proj(mems, norm_mem[l], bf(x_w_kv[l]), bm=min(512, batch * mem_len))
        xs = xattn_out(xs, q, x_q_gain[l].reshape(1, x_hd) * (x_hd ** -0.5),
                       x_k_gain[l].reshape(1, x_hd), kv, bf(x_w_out[l]),
                       bm=bm, seq=seq, mem_len=mem_len)
        xs = ffn(xs, norm_ffn[l], bf(ffn_w_up[l]), ffn_conv[l], bf(ffn_w_down[l]),
                 bm=bm, seq=seq, cf=256)
    return xs.reshape(batch, seq, d)
```

```python
import functools
import math

import jax
import jax.numpy as jnp
from jax import lax
from jax.experimental import pallas as pl
from jax.experimental.pallas import tpu as pltpu

F32 = jnp.float32
BF16 = jnp.bfloat16

EPS = 1e-6
CONV_W = 3
LANES = 128
SUBLANES = 8
BF16_SUBLANES = 16
VMEM_LIMIT = 56 * 1024 * 1024

DIFF_HEADS = 4
DIFF_HEAD_DIM = 64
SB_HEADS = 16
SB_HEAD_DIM = 64
X_HEADS = 4
LOG2E = 1.4426950408889634
MASKED = -1e30
SB_DEAD_BELOW = -120.0


def _cparams(n_axes, flags=None):
    return pltpu.CompilerParams(
        dimension_semantics=("arbitrary",) * n_axes,
        vmem_limit_bytes=VMEM_LIMIT,
        flags=flags,
    )


def _rms(x, g):
    ms = jnp.mean(x * x, axis=-1, keepdims=True)
    return x * lax.rsqrt(ms + EPS) * g


def _dot(a, b):
    return jnp.dot(a, b, preferred_element_type=F32)


def _dot_nt(a, b):
    return lax.dot_general(a, b, (((1,), (1,)), ((), ())), preferred_element_type=F32)


def _causal_conv3(y, prev, w):
    ye = jnp.concatenate([prev, y], axis=0)
    y1 = pltpu.roll(ye, 1, 0)[SUBLANES:]
    y2 = pltpu.roll(ye, 2, 0)[SUBLANES:]
    return w[2:3] * y + w[1:2] * y1 + w[0:1] * y2


def _norm_proj_kernel(x_ref, g_ref, w_ref, o_ref, *, cw):
    h = _rms(x_ref[...], g_ref[...]).astype(BF16)
    for c in range(0, w_ref.shape[1], cw):
        o_ref[:, c:c + cw] = _dot(h, w_ref[:, c:c + cw]).astype(o_ref.dtype)


def norm_proj(x, g, w, *, bm, cw=512):
    m, d = x.shape
    n = w.shape[1]
    return pl.pallas_call(
        functools.partial(_norm_proj_kernel, cw=cw),
        grid=(m // bm,),
        in_specs=[
            pl.BlockSpec((bm, d), lambda i: (i, 0)),
            pl.BlockSpec((1, d), lambda i: (0, 0)),
            pl.BlockSpec((d, n), lambda i: (0, 0)),
        ],
        out_specs=pl.BlockSpec((bm, n), lambda i: (i, 0)),
        out_shape=jax.ShapeDtypeStruct((m, n), BF16),
        compiler_params=_cparams(1),
        name="norm_proj",
    )(x, g.reshape(1, d), w)


def _even_proj_kernel(x_ref, g_ref, w_ref, cw_ref, o_ref, carry_ref, *, blocks_per_seq, cc):
    i = pl.program_id(0)
    h = _rms(x_ref[...], g_ref[...]).astype(BF16)
    a_b = _dot(h, w_ref[:, 0:cc])
    u = _dot(h, w_ref[:, cc:2 * cc]) * _dot(h, w_ref[:, 2 * cc:3 * cc])
    first = (i % blocks_per_seq) == 0
    prev = jnp.where(first, 0.0, carry_ref[...])
    o_ref[:, 0:cc] = (a_b * _causal_conv3(u, prev, cw_ref[...])).astype(o_ref.dtype)
    carry_ref[...] = u[u.shape[0] - SUBLANES:]
    for c in range(3 * cc, w_ref.shape[1], cc):
        o_ref[:, c - 2 * cc:c - cc] = _dot(h, w_ref[:, c:c + cc]).astype(o_ref.dtype)


def even_proj(x, g, w, conv_w, *, bm, seq):
    m, d = x.shape
    n = w.shape[1]
    cc = conv_w.shape[1]
    n_out = n - 2 * cc
    return pl.pallas_call(
        functools.partial(_even_proj_kernel, blocks_per_seq=seq // bm, cc=cc),
        grid=(m // bm,),
        in_specs=[
            pl.BlockSpec((bm, d), lambda i: (i, 0)),
            pl.BlockSpec((1, d), lambda i: (0, 0)),
            pl.BlockSpec((d, n), lambda i: (0, 0)),
            pl.BlockSpec((CONV_W, cc), lambda i: (0, 0)),
        ],
        out_specs=pl.BlockSpec((bm, n_out), lambda i: (i, 0)),
        out_shape=jax.ShapeDtypeStruct((m, n_out), BF16),
        scratch_shapes=[pltpu.VMEM((SUBLANES, cc), F32)],
        compiler_params=_cparams(1),
        name="even_proj",
    )(x, g.reshape(1, d), w, conv_w)


def _resid_matmul_kernel(*refs, n_lhs):
    x_ref = refs[0]
    lhs = refs[1:1 + n_lhs]
    w_ref = refs[1 + n_lhs]
    o_ref = refs[2 + n_lhs]
    acc = x_ref[...]
    k0 = 0
    for a_ref in lhs:
        k = a_ref.shape[1]
        acc = acc + _dot(a_ref[...], w_ref[k0:k0 + k, :])
        k0 += k
    o_ref[...] = acc


def resid_matmul(x, lhs_list, w, *, bm):
    m, d = x.shape
    in_specs = [pl.BlockSpec((bm, d), lambda i: (i, 0))]
    args = [x]
    for arr, cb, width in lhs_list:
        in_specs.append(pl.BlockSpec((bm, width), lambda i, cb=cb: (i, cb)))
        args.append(arr)
    in_specs.append(pl.BlockSpec(w.shape, lambda i: (0, 0)))
    args.append(w)
    return pl.pallas_call(
        functools.partial(_resid_matmul_kernel, n_lhs=len(lhs_list)),
        grid=(m // bm,),
        in_specs=in_specs,
        out_specs=pl.BlockSpec((bm, d), lambda i: (i, 0)),
        out_shape=jax.ShapeDtypeStruct((m, d), F32),
        compiler_params=_cparams(1),
        name="resid_matmul",
    )(*args)


def _half_rms(y, gain):
    lane = lax.broadcasted_iota(jnp.int32, (1, LANES), 1)
    lo = lane < DIFF_HEAD_DIM
    sq = y * y
    s_lo = jnp.sum(jnp.where(lo, sq, 0.0), axis=-1, keepdims=True)
    s_hi = jnp.sum(jnp.where(lo, 0.0, sq), axis=-1, keepdims=True)
    ms = jnp.where(lo, s_lo, s_hi) * (1.0 / DIFF_HEAD_DIM)
    return y * lax.rsqrt(ms + EPS) * gain


def _diff_attn_kernel(slopes_ref, lam_ref, qg_ref, kg_ref, sub_ref, q_ref, k_ref, v_ref,
                      o_ref, kn_ref, ve_ref, acc_ref, *, bq, lam_init, norm_rows):
    g = pl.program_id(1)
    i = pl.program_id(2)
    seq = k_ref.shape[0]
    n_heads = q_ref.shape[1] // LANES
    heads = tuple(range(n_heads))
    hcols = [slice(h * LANES, (h + 1) * LANES) for h in heads]

    @pl.when(i == 0)
    def _():
        def body(r, c):
            rows = pl.ds(pl.multiple_of(r * norm_rows, norm_rows), norm_rows)
            for h in heads:
                kn_ref[rows, hcols[h]] = _half_rms(
                    k_ref[rows, hcols[h]].astype(F32), kg_ref[...]).astype(BF16)
                ve_ref[h, rows, :] = jnp.concatenate(
                    [v_ref[rows, hcols[h]], jnp.ones((norm_rows, LANES), BF16)], axis=1)
            return c
        lax.fori_loop(0, seq // norm_rows, body, 0)

    lane = lax.broadcasted_iota(jnp.int32, (1, LANES), 1)
    lo = lane < DIFF_HEAD_DIM
    row = lax.broadcasted_iota(jnp.int32, (bq, bq), 0)
    col = lax.broadcasted_iota(jnp.int32, (bq, bq), 1)
    dist = (row - col).astype(F32)
    slopes = [slopes_ref[g * n_heads + h] * LOG2E for h in heads]
    qhs = []
    for h in heads:
        qn = _half_rms(q_ref[:, hcols[h]].astype(F32), qg_ref[...])
        qhs += [jnp.where(lo, qn, 0.0).astype(BF16), jnp.where(lo, 0.0, qn).astype(BF16)]
    maps = tuple(range(2 * n_heads))
    biases = [(-s) * dist for s in slopes]
    diag_biases = [jnp.where(row >= col, b, -jnp.inf) for b in biases]

    def tile(j, diagonal, ms):
        rows = pl.ds(pl.multiple_of(j * bq, bq), bq)
        ss = [_dot_nt(qhs[n], kn_ref[rows, hcols[n // 2]])
              + (diag_biases if diagonal else biases)[n // 2] for n in maps]
        ps, new_ms, alphas = [], [], []
        for n in maps:
            nb = (-slopes[n // 2]) * ((i - j) * bq).astype(F32)
            m_new = jnp.maximum(ms[n], jnp.max(ss[n], axis=-1, keepdims=True) + nb)
            alphas.append(jnp.exp2(ms[n] - m_new))
            ps.append(jnp.exp2((ss[n] - (m_new - nb)).astype(BF16)))
            new_ms.append(m_new)
        for n in maps:
            pv = _dot(ps[n], ve_ref[n // 2, rows, :])
            acc_ref[n] = pv if diagonal else alphas[n] * acc_ref[n] + pv
        return tuple(new_ms)

    ms = tile(i, True, (jnp.full((bq, 1), -jnp.inf, F32),) * len(maps))
    lax.fori_loop(0, i, lambda j, ms: tile(j, False, ms), ms)
    accs = [acc_ref[n] for n in maps]

    lf = lam_ref[...]
    lam = (jnp.exp(jnp.sum(lf[0:1] * lf[1:2], axis=-1, keepdims=True))
           - jnp.exp(jnp.sum(lf[2:3] * lf[3:4], axis=-1, keepdims=True)) + lam_init)
    for h in heads:
        a1, a2 = accs[2 * h], accs[2 * h + 1]
        o = a1[:, :LANES] / a1[:, LANES:] - lam * (a2[:, :LANES] / a2[:, LANES:])
        o_ref[:, hcols[h]] = (_rms(o, sub_ref[...]) * (1.0 - lam_init)).astype(o_ref.dtype)


def diff_attn(p, slopes, lam_vecs, q_gain2, k_gain2, subln, *, batch, seq, bq, lam_init,
              q_col, k_col, v_col, heads_per_step):
    nq = seq // bq
    hd = 2 * DIFF_HEAD_DIM
    width = heads_per_step * hd
    groups = DIFF_HEADS // heads_per_step
    return pl.pallas_call(
        functools.partial(_diff_attn_kernel, bq=bq, lam_init=lam_init, norm_rows=min(seq, 512)),
        grid=(batch, groups, nq),
        in_specs=[
            pl.BlockSpec(memory_space=pltpu.SMEM),
            pl.BlockSpec((4, DIFF_HEAD_DIM), lambda b, g, i: (0, 0)),
            pl.BlockSpec((1, hd), lambda b, g, i: (0, 0)),
            pl.BlockSpec((1, hd), lambda b, g, i: (0, 0)),
            pl.BlockSpec((1, hd), lambda b, g, i: (0, 0)),
            pl.BlockSpec((bq, width), lambda b, g, i: (b * nq + i, q_col // heads_per_step + g)),
            pl.BlockSpec((seq, width), lambda b, g, i: (b, k_col // heads_per_step + g)),
            pl.BlockSpec((seq, width), lambda b, g, i: (b, v_col // heads_per_step + g)),
        ],
        out_specs=pl.BlockSpec((bq, width), lambda b, g, i: (b * nq + i, g)),
        out_shape=jax.ShapeDtypeStruct((batch * seq, DIFF_HEADS * hd), BF16),
        scratch_shapes=[pltpu.VMEM((seq, width), BF16),
                        pltpu.VMEM((heads_per_step, seq, hd + LANES), BF16),
                        pltpu.VMEM((2 * heads_per_step, bq, hd + LANES), F32)],
        compiler_params=_cparams(3),
        name="diff_attn",
    )(slopes, lam_vecs, q_gain2, k_gain2, subln, p, p, p)


def _sb_attn_kernel(tri_ref, q_ref, k_ref, v_ref, o_ref, vt_ref, *, bq, scale):
    bk = tri_ref.shape[1]
    i = pl.program_id(2)
    n_slabs = vt_ref.shape[0]

    @pl.when(i == 0)
    def _():
        for j in range(n_slabs):
            vt_ref[j] = v_ref[j * bq:(j + 1) * bq, :].astype(F32).T.astype(BF16)

    lane = lax.broadcasted_iota(jnp.int32, (1, LANES), 1)
    lo = lane < SB_HEAD_DIM
    key = lax.broadcasted_iota(jnp.int32, (bq, bq), 0)
    qry = lax.broadcasted_iota(jnp.int32, (bq, bq), 1)
    mask_bias = jnp.where(key < qry, 0.0, MASKED)
    tri = tri_ref[...]

    n_heads = q_ref.shape[1] // SB_HEAD_DIM
    heads = tuple(range(n_heads))
    qs = []
    for h in heads:
        q = q_ref[:, (h // 2) * LANES:(h // 2 + 1) * LANES] * scale
        qs.append(jnp.where(lo if h % 2 == 0 else jnp.logical_not(lo), q, jnp.zeros_like(q)))

    def slab(j, bias, accs, runs):
        rows = pl.ds(pl.multiple_of(j * bq, bq), bq)
        zs = [_dot_nt(k_ref[rows, (h // 2) * LANES:(h // 2 + 1) * LANES], qs[h]) for h in heads]
        lbs, l1ms = [], []
        for h in heads:
            zb = (zs[h] if bias is None else zs[h] + bias).astype(BF16)
            lp = jnp.log(1.0 + jnp.exp(-jnp.abs(zb)))
            lb = jnp.minimum(zb, 0.0) - lp
            lbs.append(lb)
            l1ms.append(lb - zb)
        blocks = [slice(n * bk, (n + 1) * bk) for n in range(bq // bk)]
        ts = [[_dot(tri, l1ms[h][blk, :]) for blk in blocks] for h in heads]
        a_s, new_runs = [], []
        for h in heads:
            run = runs[h]
            xs = []
            for t in reversed(ts[h]):
                xs.insert(0, t[:bk, :] + run)
                run = run + t[bk:bk + 1, :]
            x = xs[0] if len(xs) == 1 else jnp.concatenate(xs, axis=0)
            a_s.append(jnp.exp(x.astype(BF16) + lbs[h]))
            new_runs.append(run)
        new_accs = [accs[h] + _dot(vt_ref[j, h * SB_HEAD_DIM:(h + 1) * SB_HEAD_DIM, :], a_s[h])
                    for h in heads]
        return tuple(new_accs), tuple(new_runs)

    zero_acc = (jnp.zeros((SB_HEAD_DIM, bq), F32),) * n_heads
    zero_run = (jnp.zeros((1, bq), F32),) * n_heads
    def any_weight_left(runs):
        return jnp.max(functools.reduce(jnp.maximum, runs)) > SB_DEAD_BELOW

    def cond(carry):
        t, live, _, _ = carry
        return jnp.logical_and(t < i, live)

    def body(carry):
        t, _, accs, runs = carry
        accs, runs = slab(i - 1 - t, None, accs, runs)
        return t + 1, any_weight_left(runs), accs, runs

    accs, runs = slab(i, mask_bias, zero_acc, zero_run)
    _, _, accs, _ = lax.while_loop(cond, body, (jnp.int32(0), any_weight_left(runs), accs, runs))
    o_ref[...] = jnp.concatenate(accs, axis=0).T.astype(o_ref.dtype)


def sb_attn(qkv, tri, *, batch, seq, bq, heads_per_step):
    nq = seq // bq
    width = heads_per_step * SB_HEAD_DIM
    groups = SB_HEADS // heads_per_step
    return pl.pallas_call(
        functools.partial(_sb_attn_kernel, bq=bq, scale=SB_HEAD_DIM ** -0.5),
        grid=(batch, groups, nq),
        in_specs=[
            pl.BlockSpec(tri.shape, lambda b, g, i: (0, 0)),
            pl.BlockSpec((bq, width), lambda b, g, i: (b * nq + i, g)),
            pl.BlockSpec((seq, width), lambda b, g, i: (b, groups + g)),
            pl.BlockSpec((seq, width), lambda b, g, i: (b, 2 * groups + g)),
        ],
        out_specs=pl.BlockSpec((bq, width), lambda b, g, i: (b * nq + i, g)),
        out_shape=jax.ShapeDtypeStruct((batch * seq, groups * width), BF16),
        scratch_shapes=[pltpu.VMEM((nq, width, bq), BF16)],
        compiler_params=_cparams(3),
        name="sb_attn",
    )(tri, qkv, qkv, qkv)


def _xattn_kernel(x_ref, q_ref, qg_ref, kg_ref, k_ref, v_ref, wo_ref, o_ref, kn_ref, oc_ref,
                  *, blocks_per_seq, hd):
    i = pl.program_id(0)

    @pl.when(i % blocks_per_seq == 0)
    def _():
        for h in range(X_HEADS):
            cols = slice(h * hd, (h + 1) * hd)
            kn_ref[:, cols] = _rms(k_ref[:, cols].astype(F32), kg_ref[...]).astype(BF16)

    hcols = [slice(h * hd, (h + 1) * hd) for h in range(X_HEADS)]
    ss = [_dot_nt(_rms(q_ref[:, c].astype(F32), qg_ref[...]).astype(BF16), kn_ref[:, c]) for c in hcols]
    ps, ls = [], []
    for s in ss:
        p = jnp.exp(s - jnp.max(s, axis=-1, keepdims=True))
        ls.append(jnp.sum(p, axis=-1, keepdims=True))
        ps.append(p.astype(BF16))
    for c, p, l in zip(hcols, ps, ls):
        oc_ref[:, c] = (_dot(p, v_ref[:, c]) / l).astype(BF16)
    o_ref[...] = x_ref[...] + _dot(oc_ref[...], wo_ref[...])


def xattn_out(x, q, q_gain, k_gain, kv, w_o, *, bm, seq, mem_len):
    m, d = x.shape
    hd = d // X_HEADS
    blocks_per_seq = seq // bm
    return pl.pallas_call(
        functools.partial(_xattn_kernel, blocks_per_seq=blocks_per_seq, hd=hd),
        grid=(m // bm,),
        in_specs=[
            pl.BlockSpec((bm, d), lambda i: (i, 0)),
            pl.BlockSpec((bm, d), lambda i: (i, 0)),
            pl.BlockSpec((1, hd), lambda i: (0, 0)),
            pl.BlockSpec((1, hd), lambda i: (0, 0)),
            pl.BlockSpec((mem_len, d), lambda i: (i // blocks_per_seq, 0)),
            pl.BlockSpec((mem_len, d), lambda i: (i // blocks_per_seq, 1)),
            pl.BlockSpec((d, d), lambda i: (0, 0)),
        ],
        out_specs=pl.BlockSpec((bm, d), lambda i: (i, 0)),
        out_shape=jax.ShapeDtypeStruct((m, d), F32),
        scratch_shapes=[pltpu.VMEM((mem_len, d), BF16), pltpu.VMEM((bm, d), BF16)],
        compiler_params=_cparams(1),
        name="xattn_out",
    )(x, q, q_gain, k_gain, kv, kv, w_o)


def _ffn_kernel(x_ref, g_ref, wu_ref, cw_ref, wd_ref, o_ref, pg_ref, pv_ref, *, blocks_per_seq, cf):
    i = pl.program_id(0)
    bm = x_ref.shape[0]
    d_ff = wd_ref.shape[0]
    first = (i % blocks_per_seq) == 0
    x = x_ref[...]
    h = _rms(x, g_ref[...]).astype(BF16)

    def up(c):
        return (_dot(h, wu_ref[:, c * cf:(c + 1) * cf]),
                _dot(h, wu_ref[:, d_ff + c * cf:d_ff + (c + 1) * cf]))

    def conv(y, col, prev_ref, c):
        prev = jnp.where(first, 0.0, prev_ref[c])
        prev_ref[c] = y[bm - SUBLANES:]
        return _causal_conv3(y, prev, cw_ref[:, col:col + cf])

    n_chunks = d_ff // cf
    acc = x
    ys = up(0)
    for c in range(n_chunks):
        ys_next = up(c + 1) if c + 1 < n_chunks else None
        gate = conv(ys[0], c * cf, pg_ref, c)
        val = conv(ys[1], d_ff + c * cf, pv_ref, c)
        act = gate * (1.0 / (1.0 + jnp.exp(-gate))) * val
        acc = acc + _dot(act.astype(BF16), wd_ref[c * cf:(c + 1) * cf, :])
        ys = ys_next
    o_ref[...] = acc


def ffn(x, g, w_up, w_conv, w_down, *, bm, seq, cf):
    m, d = x.shape
    d_ff = w_down.shape[0]
    nc = d_ff // cf
    resident = dict(pipeline_mode=pl.Buffered(1))
    return pl.pallas_call(
        functools.partial(_ffn_kernel, blocks_per_seq=seq // bm, cf=cf),
        grid=(m // bm,),
        in_specs=[
            pl.BlockSpec((bm, d), lambda i: (i, 0)),
            pl.BlockSpec((1, d), lambda i: (0, 0)),
            pl.BlockSpec((d, 2 * d_ff), lambda i: (0, 0), **resident),
            pl.BlockSpec((CONV_W, 2 * d_ff), lambda i: (0, 0), **resident),
            pl.BlockSpec((d_ff, d), lambda i: (0, 0), **resident),
        ],
        out_specs=pl.BlockSpec((bm, d), lambda i: (i, 0)),
        out_shape=jax.ShapeDtypeStruct((m, d), F32),
        scratch_shapes=[
            pltpu.VMEM((nc, SUBLANES, cf), F32),
            pltpu.VMEM((nc, SUBLANES, cf), F32),
        ],
        compiler_params=_cparams(1),
        name="ffn",
    )(x, g.reshape(1, d), w_up, w_conv, w_down)


def kernel(x, mem, norm_mix, norm_xattn, norm_mem, norm_ffn, even_w_in, even_conv, even_q_gain,
           even_k_gain, even_lambda, even_subln, even_w_out, odd_w_qkv, odd_w_out, x_w_q, x_w_kv,
           x_q_gain, x_k_gain, x_w_out, ffn_w_up, ffn_conv, ffn_w_down):
    batch, seq, d = x.shape
    mem_len = mem.shape[1]
    depth = norm_mix.shape[0]
    n = batch * seq
    bm = min(512, seq)
    bq = min(512, seq)
    bq_sb = min(256, seq)
    bk_sb = min(256, bq_sb)

    xs = x.reshape(n, d)
    mems = mem.reshape(batch * mem_len, d)
    bf = lambda a: a.astype(BF16)

    slopes = jnp.asarray([2.0 ** (-8.0 * (h + 1) / DIFF_HEADS) for h in range(DIFF_HEADS)], F32)
    tri = jnp.concatenate(
        [(lax.broadcasted_iota(jnp.int32, (bk_sb, bk_sb), 1)
          > lax.broadcasted_iota(jnp.int32, (bk_sb, bk_sb), 0)).astype(BF16),
         jnp.ones((BF16_SUBLANES, bk_sb), BF16)], axis=0)
    conv_ch = even_conv.shape[2]
    hd_diff = 2 * DIFF_HEAD_DIM
    x_hd = d // X_HEADS

    for l in range(depth):
        if l % 2 == 0:
            e = l // 2
            lam_init = 0.8 - 0.6 * math.exp(-0.3 * l)
            p = even_proj(xs, norm_mix[l], bf(even_w_in[e]), even_conv[e], bm=bm, seq=seq)
            qcol = conv_ch // hd_diff
            o = diff_attn(
                p, slopes, even_lambda[e],
                jnp.tile(even_q_gain[e], 2).reshape(1, hd_diff) * (DIFF_HEAD_DIM ** -0.5 * LOG2E),
                jnp.tile(even_k_gain[e], 2).reshape(1, hd_diff),
                even_subln[e].reshape(1, hd_diff),
                batch=batch, seq=seq, bq=bq, lam_init=lam_init,
                q_col=qcol, k_col=qcol + DIFF_HEADS, v_col=qcol + 2 * DIFF_HEADS,
                heads_per_step=2)
            xs = resid_matmul(xs, [(p, 0, conv_ch), (o, 0, o.shape[1])], bf(even_w_out[e]), bm=bm)
        else:
            k = l // 2
            qkv = norm_proj(xs, norm_mix[l], bf(odd_w_qkv[k]), bm=bm)
            o = sb_attn(qkv, tri, batch=batch, seq=seq, bq=bq_sb, heads_per_step=8)
            xs = resid_matmul(xs, [(o, 0, d)], bf(odd_w_out[k]), bm=bm)

        q = norm_proj(xs, norm_xattn[l], bf(x_w_q[l]), bm=bm)
        kv = norm_proj(mems, norm_mem[l], bf(x_w_kv[l]), bm=min(512, batch * mem_len))
        xs = xattn_out(xs, q, x_q_gain[l].reshape(1, x_hd) * (x_hd ** -0.5),
                       x_k_gain[l].reshape(1, x_hd), kv, bf(x_w_out[l]),
                       bm=bm, seq=seq, mem_len=mem_len)
        xs = ffn(xs, norm_ffn[l], bf(ffn_w_up[l]), ffn_conv[l], bf(ffn_w_down[l]),
                 bm=bm, seq=seq, cf=256)
    return xs.reshape(batch, seq, d)
```

```python
import functools
import math

import jax
import jax.numpy as jnp
from jax import lax
from jax.experimental import pallas as pl
from jax.experimental.pallas import tpu as pltpu

F32 = jnp.float32
BF16 = jnp.bfloat16

EPS = 1e-6
CONV_W = 3
LANES = 128
SUBLANES = 8
BF16_SUBLANES = 16
VMEM_LIMIT = 56 * 1024 * 1024

DIFF_HEADS = 4
DIFF_HEAD_DIM = 64
SB_HEADS = 16
SB_HEAD_DIM = 64
X_HEADS = 4
LOG2E = 1.4426950408889634
MASKED = -1e30
SB_DEAD_BELOW = -120.0


def _cparams(n_axes, flags=None):
    return pltpu.CompilerParams(
        dimension_semantics=("arbitrary",) * n_axes,
        vmem_limit_bytes=VMEM_LIMIT,
        flags=flags,
    )


def _rms(x, g):
    ms = jnp.mean(x * x, axis=-1, keepdims=True)
    return x * lax.rsqrt(ms + EPS) * g


def _dot(a, b):
    return jnp.dot(a, b, preferred_element_type=F32)


def _dot_nt(a, b):
    return lax.dot_general(a, b, (((1,), (1,)), ((), ())), preferred_element_type=F32)


def _causal_conv3(y, prev, w):
    ye = jnp.concatenate([prev, y], axis=0)
    y1 = pltpu.roll(ye, 1, 0)[SUBLANES:]
    y2 = pltpu.roll(ye, 2, 0)[SUBLANES:]
    return w[2:3] * y + w[1:2] * y1 + w[0:1] * y2


def _norm_proj_kernel(x_ref, g_ref, w_ref, o_ref, *, cw):
    h = _rms(x_ref[...], g_ref[...]).astype(BF16)
    for c in range(0, w_ref.shape[1], cw):
        o_ref[:, c:c + cw] = _dot(h, w_ref[:, c:c + cw]).astype(o_ref.dtype)


def norm_proj(x, g, w, *, bm, cw=512):
    m, d = x.shape
    n = w.shape[1]
    return pl.pallas_call(
        functools.partial(_norm_proj_kernel, cw=cw),
        grid=(m // bm,),
        in_specs=[
            pl.BlockSpec((bm, d), lambda i: (i, 0)),
            pl.BlockSpec((1, d), lambda i: (0, 0)),
            pl.BlockSpec((d, n), lambda i: (0, 0)),
        ],
        out_specs=pl.BlockSpec((bm, n), lambda i: (i, 0)),
        out_shape=jax.ShapeDtypeStruct((m, n), BF16),
        compiler_params=_cparams(1),
        name="norm_proj",
    )(x, g.reshape(1, d), w)


def _even_proj_kernel(x_ref, g_ref, w_ref, cw_ref, o_ref, carry_ref, *, blocks_per_seq, cc):
    i = pl.program_id(0)
    h = _rms(x_ref[...], g_ref[...]).astype(BF16)
    a_b = _dot(h, w_ref[:, 0:cc])
    u = _dot(h, w_ref[:, cc:2 * cc]) * _dot(h, w_ref[:, 2 * cc:3 * cc])
    first = (i % blocks_per_seq) == 0
    prev = jnp.where(first, 0.0, carry_ref[...])
    o_ref[:, 0:cc] = (a_b * _causal_conv3(u, prev, cw_ref[...])).astype(o_ref.dtype)
    carry_ref[...] = u[u.shape[0] - SUBLANES:]
    for c in range(3 * cc, w_ref.shape[1], cc):
        o_ref[:, c - 2 * cc:c - cc] = _dot(h, w_ref[:, c:c + cc]).astype(o_ref.dtype)


def even_proj(x, g, w, conv_w, *, bm, seq):
    m, d = x.shape
    n = w.shape[1]
    cc = conv_w.shape[1]
    n_out = n - 2 * cc
    return pl.pallas_call(
        functools.partial(_even_proj_kernel, blocks_per_seq=seq // bm, cc=cc),
        grid=(m // bm,),
        in_specs=[
            pl.BlockSpec((bm, d), lambda i: (i, 0)),
            pl.BlockSpec((1, d), lambda i: (0, 0)),
            pl.BlockSpec((d, n), lambda i: (0, 0)),
            pl.BlockSpec((CONV_W, cc), lambda i: (0, 0)),
        ],
        out_specs=pl.BlockSpec((bm, n_out), lambda i: (i, 0)),
        out_shape=jax.ShapeDtypeStruct((m, n_out), BF16),
        scratch_shapes=[pltpu.VMEM((SUBLANES, cc), F32)],
        compiler_params=_cparams(1),
        name="even_proj",
    )(x, g.reshape(1, d), w, conv_w)


def _resid_proj_kernel(*refs, n_lhs, cw):
    x_ref = refs[0]
    lhs = refs[1:1 + n_lhs]
    w_ref, g2_ref, w2_ref, o_ref, q_ref = refs[1 + n_lhs:]
    acc = x_ref[...]
    k0 = 0
    for a_ref in lhs:
        k = a_ref.shape[1]
        acc = acc + _dot(a_ref[...], w_ref[k0:k0 + k, :])
        k0 += k
    o_ref[...] = acc
    h = _rms(acc, g2_ref[...]).astype(BF16)
    for c in range(0, w2_ref.shape[1], cw):
        q_ref[:, c:c + cw] = _dot(h, w2_ref[:, c:c + cw]).astype(q_ref.dtype)


def resid_proj(x, lhs_list, w, g2, w2, *, bm, cw=512):
    m, d = x.shape
    n2 = w2.shape[1]
    in_specs = [pl.BlockSpec((bm, d), lambda i: (i, 0))]
    args = [x]
    for arr, cb, width in lhs_list:
        in_specs.append(pl.BlockSpec((bm, width), lambda i, cb=cb: (i, cb)))
        args.append(arr)
    in_specs += [pl.BlockSpec(w.shape, lambda i: (0, 0)),
                 pl.BlockSpec((1, d), lambda i: (0, 0)),
                 pl.BlockSpec(w2.shape, lambda i: (0, 0))]
    args += [w, g2.reshape(1, d), w2]
    return pl.pallas_call(
        functools.partial(_resid_proj_kernel, n_lhs=len(lhs_list), cw=cw),
        grid=(m // bm,),
        in_specs=in_specs,
        out_specs=[pl.BlockSpec((bm, d), lambda i: (i, 0)), pl.BlockSpec((bm, n2), lambda i: (i, 0))],
        out_shape=[jax.ShapeDtypeStruct((m, d), F32), jax.ShapeDtypeStruct((m, n2), BF16)],
        compiler_params=_cparams(1),
        name="resid_proj",
    )(*args)


def _half_rms(y, gain):
    lane = lax.broadcasted_iota(jnp.int32, (1, LANES), 1)
    lo = lane < DIFF_HEAD_DIM
    sq = y * y
    s_lo = jnp.sum(jnp.where(lo, sq, 0.0), axis=-1, keepdims=True)
    s_hi = jnp.sum(jnp.where(lo, 0.0, sq), axis=-1, keepdims=True)
    ms = jnp.where(lo, s_lo, s_hi) * (1.0 / DIFF_HEAD_DIM)
    return y * lax.rsqrt(ms + EPS) * gain


def _diff_attn_kernel(slopes_ref, lam_ref, qg_ref, kg_ref, sub_ref, q_ref, k_ref, v_ref,
                      o_ref, kn_ref, ve_ref, acc_ref, *, bq, lam_init, norm_rows):
    g = pl.program_id(1)
    i = pl.program_id(2)
    seq = k_ref.shape[0]
    n_heads = q_ref.shape[1] // LANES
    heads = tuple(range(n_heads))
    hcols = [slice(h * LANES, (h + 1) * LANES) for h in heads]

    @pl.when(i == 0)
    def _():
        def body(r, c):
            rows = pl.ds(pl.multiple_of(r * norm_rows, norm_rows), norm_rows)
            for h in heads:
                kn_ref[rows, hcols[h]] = _half_rms(
                    k_ref[rows, hcols[h]].astype(F32), kg_ref[...]).astype(BF16)
                ve_ref[h, rows, :] = jnp.concatenate(
                    [v_ref[rows, hcols[h]], jnp.ones((norm_rows, LANES), BF16)], axis=1)
            return c
        lax.fori_loop(0, seq // norm_rows, body, 0)

    lane = lax.broadcasted_iota(jnp.int32, (1, LANES), 1)
    lo = lane < DIFF_HEAD_DIM
    row = lax.broadcasted_iota(jnp.int32, (bq, bq), 0)
    col = lax.broadcasted_iota(jnp.int32, (bq, bq), 1)
    dist = (row - col).astype(F32)
    slopes = [slopes_ref[g * n_heads + h] * LOG2E for h in heads]
    qhs = []
    for h in heads:
        qn = _half_rms(q_ref[:, hcols[h]].astype(F32), qg_ref[...])
        qhs += [jnp.where(lo, qn, 0.0).astype(BF16), jnp.where(lo, 0.0, qn).astype(BF16)]
    maps = tuple(range(2 * n_heads))
    biases = [(-s) * dist for s in slopes]
    diag_biases = [jnp.where(row >= col, b, -jnp.inf) for b in biases]

    def tile(j, diagonal, ms):
        rows = pl.ds(pl.multiple_of(j * bq, bq), bq)
        ss = [_dot_nt(qhs[n], kn_ref[rows, hcols[n // 2]])
              + (diag_biases if diagonal else biases)[n // 2] for n in maps]
        ps, new_ms, alphas = [], [], []
        for n in maps:
            nb = (-slopes[n // 2]) * ((i - j) * bq).astype(F32)
            m_new = jnp.maximum(ms[n], jnp.max(ss[n], axis=-1, keepdims=True) + nb)
            alphas.append(jnp.exp2(ms[n] - m_new))
            ps.append(jnp.exp2((ss[n] - (m_new - nb)).astype(BF16)))
            new_ms.append(m_new)
        for n in maps:
            pv = _dot(ps[n], ve_ref[n // 2, rows, :])
            acc_ref[n] = pv if diagonal else alphas[n] * acc_ref[n] + pv
        return tuple(new_ms)

    ms = tile(i, True, (jnp.full((bq, 1), -jnp.inf, F32),) * len(maps))
    lax.fori_loop(0, i, lambda j, ms: tile(j, False, ms), ms)
    accs = [acc_ref[n] for n in maps]

    lf = lam_ref[...]
    lam = (jnp.exp(jnp.sum(lf[0:1] * lf[1:2], axis=-1, keepdims=True))
           - jnp.exp(jnp.sum(lf[2:3] * lf[3:4], axis=-1, keepdims=True)) + lam_init)
    for h in heads:
        a1, a2 = accs[2 * h], accs[2 * h + 1]
        o = a1[:, :LANES] / a1[:, LANES:] - lam * (a2[:, :LANES] / a2[:, LANES:])
        o_ref[:, hcols[h]] = (_rms(o, sub_ref[...]) * (1.0 - lam_init)).astype(o_ref.dtype)


def diff_attn(p, slopes, lam_vecs, q_gain2, k_gain2, subln, *, batch, seq, bq, lam_init,
              q_col, k_col, v_col, heads_per_step):
    nq = seq // bq
    hd = 2 * DIFF_HEAD_DIM
    width = heads_per_step * hd
    groups = DIFF_HEADS // heads_per_step
    return pl.pallas_call(
        functools.partial(_diff_attn_kernel, bq=bq, lam_init=lam_init, norm_rows=min(seq, 512)),
        grid=(batch, groups, nq),
        in_specs=[
            pl.BlockSpec(memory_space=pltpu.SMEM),
            pl.BlockSpec((4, DIFF_HEAD_DIM), lambda b, g, i: (0, 0)),
            pl.BlockSpec((1, hd), lambda b, g, i: (0, 0)),
            pl.BlockSpec((1, hd), lambda b, g, i: (0, 0)),
            pl.BlockSpec((1, hd), lambda b, g, i: (0, 0)),
            pl.BlockSpec((bq, width), lambda b, g, i: (b * nq + i, q_col // heads_per_step + g)),
            pl.BlockSpec((seq, width), lambda b, g, i: (b, k_col // heads_per_step + g)),
            pl.BlockSpec((seq, width), lambda b, g, i: (b, v_col // heads_per_step + g)),
        ],
        out_specs=pl.BlockSpec((bq, width), lambda b, g, i: (b * nq + i, g)),
        out_shape=jax.ShapeDtypeStruct((batch * seq, DIFF_HEADS * hd), BF16),
        scratch_shapes=[pltpu.VMEM((seq, width), BF16),
                        pltpu.VMEM((heads_per_step, seq, hd + LANES), BF16),
                        pltpu.VMEM((2 * heads_per_step, bq, hd + LANES), F32)],
        compiler_params=_cparams(3),
        name="diff_attn",
    )(slopes, lam_vecs, q_gain2, k_gain2, subln, p, p, p)


def _sb_attn_kernel(tri_ref, q_ref, k_ref, v_ref, o_ref, vt_ref, *, bq, scale):
    bk = tri_ref.shape[1]
    i = pl.program_id(2)
    n_slabs = vt_ref.shape[0]

    @pl.when(i == 0)
    def _():
        for j in range(n_slabs):
            vt_ref[j] = v_ref[j * bq:(j + 1) * bq, :].astype(F32).T.astype(BF16)

    lane = lax.broadcasted_iota(jnp.int32, (1, LANES), 1)
    lo = lane < SB_HEAD_DIM
    key = lax.broadcasted_iota(jnp.int32, (bq, bq), 0)
    qry = lax.broadcasted_iota(jnp.int32, (bq, bq), 1)
    mask_bias = jnp.where(key < qry, 0.0, MASKED)
    tri = tri_ref[...]

    n_heads = q_ref.shape[1] // SB_HEAD_DIM
    heads = tuple(range(n_heads))
    qs = []
    for h in heads:
        q = q_ref[:, (h // 2) * LANES:(h // 2 + 1) * LANES] * scale
        qs.append(jnp.where(lo if h % 2 == 0 else jnp.logical_not(lo), q, jnp.zeros_like(q)))

    def slab(j, bias, accs, runs):
        rows = pl.ds(pl.multiple_of(j * bq, bq), bq)
        zs = [_dot_nt(k_ref[rows, (h // 2) * LANES:(h // 2 + 1) * LANES], qs[h]) for h in heads]
        lbs, l1ms = [], []
        for h in heads:
            zb = (zs[h] if bias is None else zs[h] + bias).astype(BF16)
            lp = jnp.log(1.0 + jnp.exp(-jnp.abs(zb)))
            lb = jnp.minimum(zb, 0.0) - lp
            lbs.append(lb)
            l1ms.append(lb - zb)
        blocks = [slice(n * bk, (n + 1) * bk) for n in range(bq // bk)]
        ts = [[_dot(tri, l1ms[h][blk, :]) for blk in blocks] for h in heads]
        a_s, new_runs = [], []
        for h in heads:
            run = runs[h]
            xs = []
            for t in reversed(ts[h]):
                xs.insert(0, t[:bk, :] + run)
                run = run + t[bk:bk + 1, :]
            x = xs[0] if len(xs) == 1 else jnp.concatenate(xs, axis=0)
            a_s.append(jnp.exp(x.astype(BF16) + lbs[h]))
            new_runs.append(run)
        new_accs = [accs[h] + _dot(vt_ref[j, h * SB_HEAD_DIM:(h + 1) * SB_HEAD_DIM, :], a_s[h])
                    for h in heads]
        return tuple(new_accs), tuple(new_runs)

    zero_acc = (jnp.zeros((SB_HEAD_DIM, bq), F32),) * n_heads
    zero_run = (jnp.zeros((1, bq), F32),) * n_heads

    def any_weight_left(runs):
        return jnp.max(functools.reduce(jnp.maximum, runs)) > SB_DEAD_BELOW

    def cond(carry):
        t, live, _, _ = carry
        return jnp.logical_and(t < i, live)

    def body(carry):
        t, _, accs, runs = carry
        accs, runs = slab(i - 1 - t, None, accs, runs)
        return t + 1, any_weight_left(runs), accs, runs

    accs, runs = slab(i, mask_bias, zero_acc, zero_run)
    _, _, accs, _ = lax.while_loop(cond, body, (jnp.int32(0), any_weight_left(runs), accs, runs))
    o_ref[...] = jnp.concatenate(accs, axis=0).T.astype(o_ref.dtype)


def sb_attn(qkv, tri, *, batch, seq, bq, heads_per_step):
    nq = seq // bq
    width = heads_per_step * SB_HEAD_DIM
    groups = SB_HEADS // heads_per_step
    return pl.pallas_call(
        functools.partial(_sb_attn_kernel, bq=bq, scale=SB_HEAD_DIM ** -0.5),
        grid=(batch, groups, nq),
        in_specs=[
            pl.BlockSpec(tri.shape, lambda b, g, i: (0, 0)),
            pl.BlockSpec((bq, width), lambda b, g, i: (b * nq + i, g)),
            pl.BlockSpec((seq, width), lambda b, g, i: (b, groups + g)),
            pl.BlockSpec((seq, width), lambda b, g, i: (b, 2 * groups + g)),
        ],
        out_specs=pl.BlockSpec((bq, width), lambda b, g, i: (b * nq + i, g)),
        out_shape=jax.ShapeDtypeStruct((batch * seq, groups * width), BF16),
        scratch_shapes=[pltpu.VMEM((nq, width, bq), BF16)],
        compiler_params=_cparams(3),
        name="sb_attn",
    )(tri, qkv, qkv, qkv)


def _xattn_ffn_kernel(x_ref, q_ref, qg_ref, kg_ref, k_ref, v_ref, wo_ref, g_ref, wu_ref, cw_ref, wd_ref,
                      o_ref, kn_ref, oc_ref, pg_ref, pv_ref, *, blocks_per_seq, hd, cf):
    i = pl.program_id(0)
    bm = x_ref.shape[0]
    d_ff = wd_ref.shape[0]
    first = (i % blocks_per_seq) == 0

    @pl.when(first)
    def _():
        for h in range(X_HEADS):
            cols = slice(h * hd, (h + 1) * hd)
            kn_ref[:, cols] = _rms(k_ref[:, cols].astype(F32), kg_ref[...]).astype(BF16)

    hcols = [slice(h * hd, (h + 1) * hd) for h in range(X_HEADS)]
    ss = [_dot_nt(_rms(q_ref[:, c].astype(F32), qg_ref[...]).astype(BF16), kn_ref[:, c]) for c in hcols]
    ps, ls = [], []
    for s in ss:
        p = jnp.exp(s - jnp.max(s, axis=-1, keepdims=True))
        ls.append(jnp.sum(p, axis=-1, keepdims=True))
        ps.append(p.astype(BF16))
    for c, p, l in zip(hcols, ps, ls):
        oc_ref[:, c] = (_dot(p, v_ref[:, c]) / l).astype(BF16)
    x = x_ref[...] + _dot(oc_ref[...], wo_ref[...])

    h = _rms(x, g_ref[...]).astype(BF16)

    def up(c):
        return (_dot(h, wu_ref[:, c * cf:(c + 1) * cf]),
                _dot(h, wu_ref[:, d_ff + c * cf:d_ff + (c + 1) * cf]))

    def conv(y, col, prev_ref, c):
        prev = jnp.where(first, 0.0, prev_ref[c])
        prev_ref[c] = y[bm - SUBLANES:]
        return _causal_conv3(y, prev, cw_ref[:, col:col + cf])

    n_chunks = d_ff // cf
    acc = x
    ys = up(0)
    for c in range(n_chunks):
        ys_next = up(c + 1) if c + 1 < n_chunks else None
        gate = conv(ys[0], c * cf, pg_ref, c)
        val = conv(ys[1], d_ff + c * cf, pv_ref, c)
        act = gate * (1.0 / (1.0 + jnp.exp(-gate))) * val
        acc = acc + _dot(act.astype(BF16), wd_ref[c * cf:(c + 1) * cf, :])
        ys = ys_next
    o_ref[...] = acc


def xattn_ffn(x, q, q_gain, k_gain, kv, w_o, g, w_up, w_conv, w_down, *, bm, seq, mem_len, cf):
    m, d = x.shape
    hd = d // X_HEADS
    d_ff = w_down.shape[0]
    nc = d_ff // cf
    blocks_per_seq = seq // bm
    resident = dict(pipeline_mode=pl.Buffered(1))
    return pl.pallas_call(
        functools.partial(_xattn_ffn_kernel, blocks_per_seq=blocks_per_seq, hd=hd, cf=cf),
        grid=(m // bm,),
        in_specs=[
            pl.BlockSpec((bm, d), lambda i: (i, 0)),
            pl.BlockSpec((bm, d), lambda i: (i, 0)),
            pl.BlockSpec((1, hd), lambda i: (0, 0)),
            pl.BlockSpec((1, hd), lambda i: (0, 0)),
            pl.BlockSpec((mem_len, d), lambda i: (i // blocks_per_seq, 0)),
            pl.BlockSpec((mem_len, d), lambda i: (i // blocks_per_seq, 1)),
            pl.BlockSpec((d, d), lambda i: (0, 0), **resident),
            pl.BlockSpec((1, d), lambda i: (0, 0)),
            pl.BlockSpec((d, 2 * d_ff), lambda i: (0, 0), **resident),
            pl.BlockSpec((CONV_W, 2 * d_ff), lambda i: (0, 0), **resident),
            pl.BlockSpec((d_ff, d), lambda i: (0, 0), **resident),
        ],
        out_specs=pl.BlockSpec((bm, d), lambda i: (i, 0)),
        out_shape=jax.ShapeDtypeStruct((m, d), F32),
        scratch_shapes=[
            pltpu.VMEM((mem_len, d), BF16),
            pltpu.VMEM((bm, d), BF16),
            pltpu.VMEM((nc, SUBLANES, cf), F32),
            pltpu.VMEM((nc, SUBLANES, cf), F32),
        ],
        compiler_params=_cparams(1),
        name="xattn_ffn",
    )(x, q, q_gain, k_gain, kv, kv, w_o, g.reshape(1, d), w_up, w_conv, w_down)


def kernel(x, mem, norm_mix, norm_xattn, norm_mem, norm_ffn, even_w_in, even_conv, even_q_gain,
           even_k_gain, even_lambda, even_subln, even_w_out, odd_w_qkv, odd_w_out, x_w_q, x_w_kv,
           x_q_gain, x_k_gain, x_w_out, ffn_w_up, ffn_conv, ffn_w_down):
    batch, seq, d = x.shape
    mem_len = mem.shape[1]
    depth = norm_mix.shape[0]
    n = batch * seq
    bm = min(512, seq)
    bm_proj = min(1024, seq)
    bq = min(512, seq)
    bq_sb = min(256, seq)
    bk_sb = min(256, bq_sb)

    xs = x.reshape(n, d)
    mems = mem.reshape(batch * mem_len, d)
    bf = lambda a: a.astype(BF16)

    slopes = jnp.asarray([2.0 ** (-8.0 * (h + 1) / DIFF_HEADS) for h in range(DIFF_HEADS)], F32)
    tri = jnp.concatenate(
        [(lax.broadcasted_iota(jnp.int32, (bk_sb, bk_sb), 1)
          > lax.broadcasted_iota(jnp.int32, (bk_sb, bk_sb), 0)).astype(BF16),
         jnp.ones((BF16_SUBLANES, bk_sb), BF16)], axis=0)
    conv_ch = even_conv.shape[2]
    hd_diff = 2 * DIFF_HEAD_DIM
    x_hd = d // X_HEADS

    for l in range(depth):
        if l % 2 == 0:
            e = l // 2
            lam_init = 0.8 - 0.6 * math.exp(-0.3 * l)
            p = even_proj(xs, norm_mix[l], bf(even_w_in[e]), even_conv[e], bm=bm_proj, seq=seq)
            qcol = conv_ch // hd_diff
            o = diff_attn(
                p, slopes, even_lambda[e],
                jnp.tile(even_q_gain[e], 2).reshape(1, hd_diff) * (DIFF_HEAD_DIM ** -0.5 * LOG2E),
                jnp.tile(even_k_gain[e], 2).reshape(1, hd_diff),
                even_subln[e].reshape(1, hd_diff),
                batch=batch, seq=seq, bq=bq, lam_init=lam_init,
                q_col=qcol, k_col=qcol + DIFF_HEADS, v_col=qcol + 2 * DIFF_HEADS,
                heads_per_step=2)
            mix_lhs, mix_w = [(p, 0, conv_ch), (o, 0, o.shape[1])], even_w_out[e]
        else:
            k = l // 2
            qkv = norm_proj(xs, norm_mix[l], bf(odd_w_qkv[k]), bm=bm_proj)
            o = sb_attn(qkv, tri, batch=batch, seq=seq, bq=bq_sb, heads_per_step=8)
            mix_lhs, mix_w = [(o, 0, d)], odd_w_out[k]

        xs, q = resid_proj(xs, mix_lhs, bf(mix_w), norm_xattn[l], bf(x_w_q[l]), bm=bm_proj)
        kv = norm_proj(mems, norm_mem[l], bf(x_w_kv[l]), bm=min(512, batch * mem_len))
        xs = xattn_ffn(xs, q, x_q_gain[l].reshape(1, x_hd) * (x_hd ** -0.5),
                       x_k_gain[l].reshape(1, x_hd), kv, bf(x_w_out[l]),
                       norm_ffn[l], bf(ffn_w_up[l]), ffn_conv[l], bf(ffn_w_down[l]),
                       bm=bm, seq=seq, mem_len=mem_len, cf=256)
    return xs.reshape(batch, seq, d)
```
